```python
import jax, jax.numpy as jnp
from jax import lax
import numpy as np

D_MODEL = 1024
BATCH = 16
SEQ = 256
DEPTH = 4
DEC_BATCH = 4
DEC_SEQ = 4096
PAST_LEN = 256

GRID_W = 64
D_FF = ((8 * D_MODEL // 3 + 255) // 256) * 256
FFN_RESIDUAL = 0.5
N_MOD = 9
A_WIDTH = D_MODEL // 2
POOL_WINDOWS = (2, 4, 8, 16)
POOL_GROUP = A_WIDTH // len(POOL_WINDOWS)
NB_HEAD_DIM = 64
NB_HEADS = (D_MODEL // 2) // NB_HEAD_DIM
NB_ROWS = 8
NB_COLS = 16
C_HEAD_DIM = 64
C_Q_HEADS = D_MODEL // C_HEAD_DIM
C_KV_HEADS = C_Q_HEADS // 4
Q_BLOCK = 128
ROPE_THETA = 10000.0
EPS = 1e-6
NEG_INF = -1e30

kernel_name = "hybrid_dit_pool_natten_gqa_step"


def rms_norm(x, g):
    x32 = x.astype(jnp.float32)
    y = x32 * lax.rsqrt(jnp.mean(x32 * x32, axis=-1, keepdims=True) + EPS)
    return (y * g.astype(jnp.float32)).astype(x.dtype)


def modulate(h, shift, scale):
    return h * (1 + scale) + shift


def ada(cvec, w_mod, b_mod):
    m = jax.nn.silu(cvec) @ w_mod + b_mod
    if m.ndim == 2:
        m = m[:, None, :]
    return jnp.split(m, N_MOD, axis=-1)


def ffn_sub(x, g, shift, scale, gate, w_in, w_out):
    h = modulate(rms_norm(x, g), shift, scale)
    a, b = jnp.split(h @ w_in, 2, axis=-1)
    return x + gate * (FFN_RESIDUAL * ((jax.nn.silu(a) * b) @ w_out))


def head_rms(x, g):
    x32 = x.astype(jnp.float32)
    y = x32 * lax.rsqrt(jnp.mean(x32 * x32, axis=-1, keepdims=True) + EPS)
    return (y * g.astype(jnp.float32)).astype(x.dtype)


def _rotate(xa, pos):
    n = xa.shape[-1] // 2
    inv = ROPE_THETA ** (-jnp.arange(n, dtype=jnp.float32) / n)
    ang = pos[:, None] * inv[None, :]
    cos = jnp.cos(ang)[None, :, None, :]
    sin = jnp.sin(ang)[None, :, None, :]
    x1, x2 = xa[..., :n], xa[..., n:]
    return jnp.concatenate([x1 * cos - x2 * sin, x2 * cos + x1 * sin], axis=-1)


def rope_2d(x):
    L, D = x.shape[1], x.shape[-1]
    t = jnp.arange(L)
    x32 = x.astype(jnp.float32)
    half = D // 2
    xr = _rotate(x32[..., :half], (t // GRID_W).astype(jnp.float32))
    xc = _rotate(x32[..., half:], (t % GRID_W).astype(jnp.float32))
    return jnp.concatenate([xr, xc], axis=-1).astype(x.dtype)


def blocked_attention(q, k, v):
    B, Sq, Hq, D = q.shape
    Hkv = k.shape[2]
    G = Hq // Hkv
    nblk = Sq // Q_BLOCK
    qb = (q * (D ** -0.5)).reshape(B, nblk, Q_BLOCK, Hkv, G, D)
    qb = jnp.moveaxis(qb, 1, 0)

    def blk(qi):
        s = jnp.einsum('bqhgd,bkhd->bhgqk', qi, k).astype(jnp.float32)
        p = jax.nn.softmax(s, axis=-1).astype(v.dtype)
        return jnp.einsum('bhgqk,bkhd->bqhgd', p, v)

    o = lax.map(blk, qb)
    return jnp.moveaxis(o, 0, 1).reshape(B, Sq, Hq, D)


def pool_mix(u, w_pool, pool_scale):
    B, L, _ = u.shape
    u32 = u.astype(jnp.float32)
    t = jnp.arange(L)
    outs = []
    for gi, w in enumerate(POOL_WINDOWS):
        ug = u32[..., gi * POOL_GROUP:(gi + 1) * POOL_GROUP]
        csum = jnp.concatenate([jnp.zeros((B, 1, POOL_GROUP), jnp.float32), jnp.cumsum(ug, axis=1)], axis=1)
        lo = jnp.clip(t - w // 2, 0, L - 1)
        hi = jnp.clip(t + (w - 1 - w // 2), 0, L - 1)
        cnt = (hi - lo + 1).astype(jnp.float32)
        s = jnp.take(csum, hi + 1, axis=1) - jnp.take(csum, lo, axis=1)
        outs.append(s / cnt[None, :, None] - ug)
    pooled = jnp.stack(outs, axis=2).astype(u.dtype)
    mixed = jnp.einsum('blgc,gcd->blgd', pooled, w_pool).reshape(B, L, A_WIDTH)
    return mixed * pool_scale


def neighbourhood_attention(q, k, v, ctx_k, ctx_v, rpb):
    B, L, H, Dh = q.shape
    rows = L // GRID_W
    kh = min(NB_ROWS, rows)
    nb = kh * GRID_W
    qg = (q * (Dh ** -0.5)).reshape(B, rows, GRID_W, H, Dh)
    kg = k.reshape(B, rows, GRID_W, H, Dh)
    vg = v.reshape(B, rows, GRID_W, H, Dh)
    cq = jnp.arange(GRID_W)
    cstart = jnp.clip(cq - NB_COLS // 2, 0, GRID_W - NB_COLS)
    col_valid = (cq[None, :] >= cstart[:, None]) & (cq[None, :] < cstart[:, None] + NB_COLS)
    col_idx = jnp.clip(cq[None, :] - cq[:, None] + NB_COLS - 1, 0, 2 * NB_COLS - 2)

    def row_block(r):
        rs = jnp.clip(r - kh // 2, 0, rows - kh)
        kb = lax.dynamic_slice_in_dim(kg, rs, kh, axis=1)
        vb = lax.dynamic_slice_in_dim(vg, rs, kh, axis=1)
        qr = lax.dynamic_index_in_dim(qg, r, axis=1, keepdims=False)
        s_nb = jnp.einsum('bqhd,bkwhd->bhqkw', qr, kb).astype(jnp.float32)
        dr = rs + jnp.arange(kh) - r + NB_ROWS - 1
        bias = rpb[:, dr[:, None, None], col_idx[None, :, :]]
        bias = jnp.transpose(bias, (0, 2, 1, 3)).astype(jnp.float32)
        s_nb = jnp.where(col_valid[:, None, :], s_nb + bias, NEG_INF)
        s_ctx = jnp.einsum('bqhd,bchd->bhqc', qr, ctx_k).astype(jnp.float32)
        s = jnp.concatenate([s_nb.reshape(B, H, GRID_W, nb), s_ctx], axis=-1)
        p = jax.nn.softmax(s, axis=-1).astype(v.dtype)
        p_nb = p[..., :nb].reshape(B, H, GRID_W, kh, GRID_W)
        return (jnp.einsum('bhqkw,bkwhd->bqhd', p_nb, vb)
                + jnp.einsum('bhqc,bchd->bqhd', p[..., nb:], ctx_v))

    out = lax.map(row_block, jnp.arange(rows))
    return jnp.moveaxis(out, 0, 1).reshape(B, L, H, Dh)


def mixer_ab(h, w_in, w_pool, pool_scale, rpb, w_out, ctx_k=None, ctx_v=None):
    B, L, _ = h.shape
    proj = h @ w_in
    u = proj[..., :A_WIDTH]
    q, k, v = [t.reshape(B, L, NB_HEADS, NB_HEAD_DIM) for t in jnp.split(proj[..., A_WIDTH:], 3, axis=-1)]
    a_out = pool_mix(u, w_pool, pool_scale)
    if ctx_k is None:
        b_out = blocked_attention(q, k, v)
    else:
        b_out = neighbourhood_attention(q, k, v, ctx_k, ctx_v, rpb)
    out = jnp.concatenate([a_out, b_out.reshape(B, L, NB_HEADS * NB_HEAD_DIM)], axis=-1) @ w_out
    return out, k, v


def mixer_c(h, w_qkv, g_q, g_k, w_out, ctx_k=None, ctx_v=None):
    B, L, _ = h.shape
    proj = h @ w_qkv
    nq = C_Q_HEADS * C_HEAD_DIM
    nk = C_KV_HEADS * C_HEAD_DIM
    q = head_rms(proj[..., :nq].reshape(B, L, C_Q_HEADS, C_HEAD_DIM), g_q)
    k = head_rms(proj[..., nq:nq + nk].reshape(B, L, C_KV_HEADS, C_HEAD_DIM), g_k)
    v = proj[..., nq + nk:].reshape(B, L, C_KV_HEADS, C_HEAD_DIM)
    if ctx_k is None:
        o = blocked_attention(q, k, v)
    else:
        o = blocked_attention(rope_2d(q), jnp.concatenate([ctx_k, rope_2d(k)], axis=1),
                              jnp.concatenate([ctx_v, v], axis=1))
    return o.reshape(B, L, nq) @ w_out, k, v


def layer(p, l, x, cvec, ctx_k=None, ctx_v=None):
    sh1, sc1, g1, sh2, sc2, g2, sh3, sc3, g3 = ada(cvec, p['w_mod'][l], p['b_mod'][l])
    x = ffn_sub(x, p['g_norm'][l, 0], sh1, sc1, g1, p['w_ffn_in'][l, 0], p['w_ffn_out'][l, 0])
    h = modulate(rms_norm(x, p['g_norm'][l, 1]), sh2, sc2)
    if l % 2 == 0:
        e = l // 2
        mix, k, v = mixer_ab(h, p['w_in_ab'][e], p['w_pool'][e], p['pool_scale'][e], p['nb_rpb'][e],
                             p['w_out_ab'][e], ctx_k, ctx_v)
    else:
        o = l // 2
        mix, k, v = mixer_c(h, p['w_qkv_c'][o], p['g_qnorm'][o], p['g_knorm'][o], p['w_out_c'][o],
                            ctx_k, ctx_v)
    x = x + g2 * mix
    x = ffn_sub(x, p['g_norm'][l, 2], sh3, sc3, g3, p['w_ffn_in'][l, 1], p['w_ffn_out'][l, 1])
    return x, k, v


def setup_inputs(seed: int = 0) -> dict:
    key = jax.random.key(seed)
    ks = jax.random.split(key, 24)
    f32 = jnp.float32
    n_even = (DEPTH + 1) // 2
    n_odd = DEPTH // 2

    def nrm(k, shape, scale):
        return jax.random.normal(k, shape, f32) * scale

    mix_ab = A_WIDTH + NB_HEADS * NB_HEAD_DIM
    qkv_c = (C_Q_HEADS + 2 * C_KV_HEADS) * C_HEAD_DIM
    return {
        'x_prompt': nrm(ks[0], (BATCH, SEQ, D_MODEL), 1.0),
        'x_sample': nrm(ks[1], (DEC_BATCH, DEC_SEQ, D_MODEL), 1.0),
        'cache_nb_k': nrm(ks[2], (DEC_BATCH, n_even, PAST_LEN, NB_HEADS, NB_HEAD_DIM), 1.0),
        'cache_nb_v': nrm(ks[3], (DEC_BATCH, n_even, PAST_LEN, NB_HEADS, NB_HEAD_DIM), 1.0),
        'cache_attn_k': nrm(ks[4], (DEC_BATCH, n_odd, PAST_LEN, C_KV_HEADS, C_HEAD_DIM), 1.0),
        'cache_attn_v': nrm(ks[5], (DEC_BATCH, n_odd, PAST_LEN, C_KV_HEADS, C_HEAD_DIM), 1.0),
        'c': nrm(ks[6], (DEC_BATCH, D_MODEL), 1.0),
        'c_ctx': nrm(ks[7], (D_MODEL,), 1.0),
        'w_mod': nrm(ks[8], (DEPTH, D_MODEL, N_MOD * D_MODEL), 0.5 * D_MODEL ** -0.5),
        'b_mod': nrm(ks[9], (DEPTH, N_MOD * D_MODEL), 0.01),
        'g_norm': 1.0 + nrm(ks[10], (DEPTH, 3, D_MODEL), 0.1),
        'w_ffn_in': nrm(ks[11], (DEPTH, 2, D_MODEL, 2 * D_FF), D_MODEL ** -0.5),
        'w_ffn_out': nrm(ks[12], (DEPTH, 2, D_FF, D_MODEL), D_FF ** -0.5),
        'w_in_ab': nrm(ks[13], (n_even, D_MODEL, A_WIDTH + 3 * NB_HEADS * NB_HEAD_DIM), D_MODEL ** -0.5),
        'w_pool': nrm(ks[14], (n_even, len(POOL_WINDOWS), POOL_GROUP, POOL_GROUP), POOL_GROUP ** -0.5),
        'pool_scale': 1.0 + nrm(ks[15], (n_even, A_WIDTH), 0.1),
        'nb_rpb': nrm(ks[16], (n_even, NB_HEADS, 2 * NB_ROWS - 1, 2 * NB_COLS - 1), 0.1),
        'w_out_ab': nrm(ks[17], (n_even, mix_ab, D_MODEL), mix_ab ** -0.5),
        'w_qkv_c': nrm(ks[18], (n_odd, D_MODEL, qkv_c), D_MODEL ** -0.5),
        'g_qnorm': 1.0 + nrm(ks[19], (n_odd, C_HEAD_DIM), 0.1),
        'g_knorm': 1.0 + nrm(ks[20], (n_odd, C_HEAD_DIM), 0.1),
        'w_out_c': nrm(ks[21], (n_odd, C_Q_HEADS * C_HEAD_DIM, D_MODEL), (C_Q_HEADS * C_HEAD_DIM) ** -0.5),
        'g_final': 1.0 + nrm(ks[22], (D_MODEL,), 0.1),
    }


def reference(x_prompt, x_sample, cache_nb_k, cache_nb_v, cache_attn_k, cache_attn_v, c, c_ctx,
              w_mod, b_mod, g_norm, w_ffn_in, w_ffn_out, w_in_ab, w_pool, pool_scale, nb_rpb, w_out_ab,
              w_qkv_c, g_qnorm, g_knorm, w_out_c, g_final):
    p = {'w_mod': w_mod, 'b_mod': b_mod, 'g_norm': g_norm, 'w_ffn_in': w_ffn_in, 'w_ffn_out': w_ffn_out,
         'w_in_ab': w_in_ab, 'w_pool': w_pool, 'pool_scale': pool_scale, 'nb_rpb': nb_rpb,
         'w_out_ab': w_out_ab, 'w_qkv_c': w_qkv_c, 'g_qnorm': g_qnorm, 'g_knorm': g_knorm,
         'w_out_c': w_out_c}

    xp = x_prompt
    nb_k, nb_v, at_k, at_v = [], [], [], []
    for l in range(DEPTH):
        xp, k, v = layer(p, l, xp, c_ctx)
        if l % 2 == 0:
            nb_k.append(k)
            nb_v.append(v)
        else:
            at_k.append(k)
            at_v.append(v)
    y_prompt = rms_norm(xp, g_final)

    xs = x_sample
    for l in range(DEPTH):
        if l % 2 == 0:
            ck, cv = cache_nb_k[:, l // 2], cache_nb_v[:, l // 2]
        else:
            ck, cv = cache_attn_k[:, l // 2], cache_attn_v[:, l // 2]
        xs, _, _ = layer(p, l, xs, c, ck, cv)
    y_sample = rms_norm(xs, g_final)

    new_nb_k = jnp.stack(nb_k, axis=1)
    new_nb_v = jnp.stack(nb_v, axis=1)
    new_attn_k = jnp.stack(at_k, axis=1)
    new_attn_v = jnp.stack(at_v, axis=1)
    return (y_prompt, y_sample, new_nb_k, new_nb_v, new_attn_k, new_attn_v)
```

```python
import functools

import numpy as np
import jax
import jax.numpy as jnp
from jax import lax
from jax.experimental import pallas as pl
from jax.experimental.pallas import tpu as pltpu

F32 = jnp.float32
BF16 = jnp.bfloat16

D_MODEL = 1024
DEPTH = 4
GRID_W = 64
D_FF = 2816
N_MOD = 9
A_WIDTH = 512
POOL_WINDOWS = (2, 4, 8, 16)
POOL_GROUP = 128
HEAD_DIM = 64
NB_HEADS = 8
NB_ROWS = 8
NB_COLS = 16
C_Q_HEADS = 16
C_KV_HEADS = 4
ROPE_THETA = 10000.0
EPS = 1e-6
NEG_INF = -1e30
Q_SCALE = HEAD_DIM ** -0.5

LANES = 128
SUBLANES = 8
VMEM_LIMIT_BYTES = 56 * 1024 * 1024

TM = 512
TF = 256
TPOOL = 256
POOL_HALO = 8
NB_BAND = 8
NB_KROWS = 16
TQ = 256


def _cparams(*sem):
    return pltpu.CompilerParams(dimension_semantics=sem, vmem_limit_bytes=VMEM_LIMIT_BYTES)


def _resident(shape, index_map):
    return pl.BlockSpec(shape, index_map, pipeline_mode=pl.Buffered(1))


def _dot(a, b):
    return jnp.dot(a, b, preferred_element_type=F32)


def _dot_nt(a, b):
    return lax.dot_general(a, b, (((1,), (1,)), ((), ())), preferred_element_type=F32)


def _norm_mod(x, g, shift, scale):
    ms = jnp.mean(x * x, axis=-1, keepdims=True)
    y = x * lax.rsqrt(ms + EPS) * g
    return y * (1.0 + scale) + shift


def _ada_kernel(c_ref, w_ref, b_ref, o_ref):
    c = c_ref[...]
    s = (c * jax.nn.sigmoid(c)).astype(BF16)
    o_ref[...] = _dot(s, w_ref[...].astype(BF16)) + b_ref[...]


def _ada(cvec, w_mod, b_mod):
    rows = cvec.shape[0]
    n = w_mod.shape[-1]
    tn = D_MODEL
    return pl.pallas_call(
        _ada_kernel,
        out_shape=jax.ShapeDtypeStruct((DEPTH, rows, n), F32),
        grid=(DEPTH, n // tn),
        in_specs=[
            pl.BlockSpec((rows, D_MODEL), lambda l, j: (0, 0)),
            pl.BlockSpec((None, D_MODEL, tn), lambda l, j: (l, 0, j)),
            pl.BlockSpec((None, 1, tn), lambda l, j: (l, 0, j)),
        ],
        out_specs=pl.BlockSpec((None, rows, tn), lambda l, j: (l, 0, j)),
        compiler_params=_cparams("arbitrary", "arbitrary"),
        name="ada_mod",
    )(cvec, w_mod, b_mod.reshape(DEPTH, 1, n))


class _Tokens:
    def __init__(self, bp, lp, bs, ls):
        self.bp, self.lp, self.bs, self.ls = bp, lp, bs, ls
        self.tp = bp * lp
        self.ts = bs * ls
        self.t = self.tp + self.ts
        assert self.tp % TM == 0 and ls % TM == 0 and lp % TPOOL == 0 and ls % TPOOL == 0
        self.npt = self.tp // TM
        self.nt = self.t // TM
        self.tiles_per_seq = ls // TM

    def mod_row(self, i):
        return jnp.where(i < self.npt, 0, 1 + (i - self.npt) // self.tiles_per_seq)

    def x_spec(self, width=D_MODEL):
        return pl.BlockSpec((TM, width), lambda i: (i, 0))

    def mod_spec(self):
        return pl.BlockSpec((None, N_MOD, D_MODEL), lambda i: (self.mod_row(i), 0, 0))

    def prompt_spec(self, width):
        return pl.BlockSpec((TM, width), lambda i: (jnp.minimum(i, self.npt - 1), 0))

    def sample_spec(self, width):
        return pl.BlockSpec((TM, width), lambda i: (jnp.maximum(i - self.npt, 0), 0))


def _ffn_kernel(x_ref, mod_ref, g_ref, win_ref, wout_ref, *rest, sub, final):
    if final:
        gf_ref, o_ref, act_ref = rest
    else:
        o_ref, act_ref = rest
    x = x_ref[...]
    h = _norm_mod(x, g_ref[...], mod_ref[3 * sub:3 * sub + 1, :], mod_ref[3 * sub + 1:3 * sub + 2, :]).astype(BF16)
    for j in range(D_FF // TF):
        a = _dot(h, win_ref[:, j * TF:(j + 1) * TF])
        b = _dot(h, win_ref[:, D_FF + j * TF:D_FF + (j + 1) * TF])
        act_ref[:, j * TF:(j + 1) * TF] = (a * jax.nn.sigmoid(a) * b).astype(BF16)
    y = x + mod_ref[3 * sub + 2:3 * sub + 3, :] * (0.5 * _dot(act_ref[...], wout_ref[...]))
    if final:
        ms = jnp.mean(y * y, axis=-1, keepdims=True)
        y = y * lax.rsqrt(ms + EPS) * gf_ref[...]
    o_ref[...] = y


def _ffn(tok, x, mod_l, g, w_in, w_out, sub, g_final=None):
    final = g_final is not None
    in_specs = [
        tok.x_spec(), tok.mod_spec(),
        pl.BlockSpec((1, D_MODEL), lambda i: (0, 0)),
        _resident((D_MODEL, 2 * D_FF), lambda i: (0, 0)),
        _resident((D_FF, D_MODEL), lambda i: (0, 0)),
    ]
    args = [x, mod_l, g.reshape(1, D_MODEL), w_in, w_out]
    if final:
        in_specs.append(pl.BlockSpec((1, D_MODEL), lambda i: (0, 0)))
        args.append(g_final.reshape(1, D_MODEL))
    return pl.pallas_call(
        functools.partial(_ffn_kernel, sub=sub, final=final),
        out_shape=jax.ShapeDtypeStruct((tok.t, D_MODEL), F32),
        grid=(tok.nt,),
        in_specs=in_specs,
        out_specs=tok.x_spec(),
        scratch_shapes=[pltpu.VMEM((TM, D_FF), BF16)],
        compiler_params=_cparams("arbitrary"),
        name="ffn_final" if final else "ffn",
    )(*args)


def _proj_ab_kernel(x_ref, mod_ref, g_ref, w_ref, u_ref, q_ref, k_ref, v_ref, kf_ref, vf_ref, *, npt):
    i = pl.program_id(0)
    h = _norm_mod(x_ref[...], g_ref[...], mod_ref[3:4, :], mod_ref[4:5, :]).astype(BF16)
    proj = _dot(h, w_ref[...])
    u_ref[...] = proj[:, :A_WIDTH]
    hw = NB_HEADS * HEAD_DIM
    for hh in range(NB_HEADS):
        lo = A_WIDTH + hh * HEAD_DIM
        q_ref[hh] = (proj[:, lo:lo + HEAD_DIM] * Q_SCALE).astype(BF16)
        k_ref[hh] = proj[:, lo + hw:lo + hw + HEAD_DIM].astype(BF16)
        v_ref[hh] = proj[:, lo + 2 * hw:lo + 2 * hw + HEAD_DIM].astype(BF16)

    @pl.when(i < npt)
    def _():
        kf_ref[...] = proj[:, A_WIDTH + hw:A_WIDTH + 2 * hw]
        vf_ref[...] = proj[:, A_WIDTH + 2 * hw:A_WIDTH + 3 * hw]


def _proj_ab(tok, x, mod_l, g, w):
    hw = NB_HEADS * HEAD_DIM
    head_spec = pl.BlockSpec((NB_HEADS, TM, HEAD_DIM), lambda i: (0, i, 0))
    head_shape = jax.ShapeDtypeStruct((NB_HEADS, tok.t, HEAD_DIM), BF16)
    return pl.pallas_call(
        functools.partial(_proj_ab_kernel, npt=tok.npt),
        out_shape=(jax.ShapeDtypeStruct((tok.t, A_WIDTH), F32), head_shape, head_shape, head_shape,
                   jax.ShapeDtypeStruct((tok.tp, hw), F32), jax.ShapeDtypeStruct((tok.tp, hw), F32)),
        grid=(tok.nt,),
        in_specs=[tok.x_spec(), tok.mod_spec(), pl.BlockSpec((1, D_MODEL), lambda i: (0, 0)),
                  _resident(w.shape, lambda i: (0, 0))],
        out_specs=(tok.x_spec(A_WIDTH), head_spec, head_spec, head_spec,
                   tok.prompt_spec(hw), tok.prompt_spec(hw)),
        compiler_params=_cparams("arbitrary"),
        name="proj_ab",
    )(x, mod_l, g.reshape(1, D_MODEL), w)


def _pool_kernel(u_ref, prev_ref, next_ref, w_ref, sc_ref, o_ref, *, n_prompt_tiles, lp, ls):
    i = pl.program_id(0)
    is_p = i < n_prompt_tiles
    tiles_s = ls // TPOOL
    j = jnp.where(is_p, 0, (i - n_prompt_tiles) % tiles_s)
    seq_len = jnp.where(is_p, lp, ls)
    pos0 = j * TPOOL
    first = j == 0
    last = pos0 + TPOOL == seq_len
    prev = jnp.where(first, 0.0, prev_ref[...])
    nxt = jnp.where(last, 0.0, next_ref[...])
    ext = jnp.concatenate([prev, u_ref[...], nxt], axis=0)
    n_ext = TPOOL + 2 * POOL_HALO
    pos = pos0 + lax.broadcasted_iota(jnp.int32, (TPOOL, 1), 0)

    def fwd(v, d):
        return pltpu.roll(v, n_ext - d, 0)

    for gi, w in enumerate(POOL_WINDOWS):
        e = ext[:, gi * POOL_GROUP:(gi + 1) * POOL_GROUP]
        acc = e
        span = 1
        while span < w:
            acc = acc + fwd(acc, span)
            span *= 2
        half = w // 2
        centred = pltpu.roll(acc, half, 0)
        s = centred[POOL_HALO:POOL_HALO + TPOOL]
        cnt = (jnp.minimum(pos + (w - 1 - half), seq_len - 1) - jnp.maximum(pos - half, 0) + 1).astype(F32)
        pooled = (s / cnt - e[POOL_HALO:POOL_HALO + TPOOL]).astype(BF16)
        mixed = _dot(pooled, w_ref[gi])
        o_ref[:, gi * POOL_GROUP:(gi + 1) * POOL_GROUP] = (
            mixed * sc_ref[:, gi * POOL_GROUP:(gi + 1) * POOL_GROUP]).astype(BF16)


def _pool(tok, u, w_pool, pool_scale):
    n_tiles = tok.t // TPOOL
    per = TPOOL // POOL_HALO
    n_halo_blocks = tok.t // POOL_HALO
    return pl.pallas_call(
        functools.partial(_pool_kernel, n_prompt_tiles=tok.tp // TPOOL, lp=tok.lp, ls=tok.ls),
        out_shape=jax.ShapeDtypeStruct((tok.t, A_WIDTH), BF16),
        grid=(n_tiles,),
        in_specs=[
            pl.BlockSpec((TPOOL, A_WIDTH), lambda i: (i, 0)),
            pl.BlockSpec((POOL_HALO, A_WIDTH), lambda i: (jnp.maximum(i * per - 1, 0), 0)),
            pl.BlockSpec((POOL_HALO, A_WIDTH), lambda i: (jnp.minimum((i + 1) * per, n_halo_blocks - 1), 0)),
            pl.BlockSpec((len(POOL_WINDOWS), POOL_GROUP, POOL_GROUP), lambda i: (0, 0, 0)),
            pl.BlockSpec((1, A_WIDTH), lambda i: (0, 0)),
        ],
        out_specs=pl.BlockSpec((TPOOL, A_WIDTH), lambda i: (i, 0)),
        compiler_params=_cparams("arbitrary"),
        name="pool_mix",
    )(u, u, u, w_pool, pool_scale.reshape(1, A_WIDTH))


def _attend(q, sources):
    scores = []
    for k, _, bias in sources:
        s = _dot_nt(q, k)
        if bias is not None:
            s = s + bias
        scores.append(s)
    m = functools.reduce(jnp.maximum, [s.max(axis=-1, keepdims=True) for s in scores])
    den = None
    out = None
    for s, (_, v, _) in zip(scores, sources):
        p = jnp.exp(s - m)
        ps = p.sum(axis=-1, keepdims=True)
        pv = _dot(p.astype(BF16), v)
        den = ps if den is None else den + ps
        out = pv if out is None else out + pv
    return out / den


def _attn_self_kernel(q_ref, k_ref, v_ref, o_ref, *, shared_kv):
    outs = []
    for hh in range(2):
        kv = 0 if shared_kv else hh
        outs.append(_attend(q_ref[hh], [(k_ref[kv], v_ref[kv], None)]))
    o_ref[...] = jnp.concatenate(outs, axis=-1).astype(BF16)


def _prompt_attn(tok, q, k, v, n_q_heads, n_kv_heads):
    group = n_q_heads // n_kv_heads
    shared = group > 1
    kv_block = 1 if shared else 2
    kv_idx = (lambda b, p: (2 * p // group, b, 0)) if shared else (lambda b, p: (p, b, 0))
    return pl.pallas_call(
        functools.partial(_attn_self_kernel, shared_kv=shared),
        out_shape=jax.ShapeDtypeStruct((tok.tp, n_q_heads * HEAD_DIM), BF16),
        grid=(tok.bp, n_q_heads // 2),
        in_specs=[
            pl.BlockSpec((2, tok.lp, HEAD_DIM), lambda b, p: (p, b, 0)),
            pl.BlockSpec((kv_block, tok.lp, HEAD_DIM), kv_idx),
            pl.BlockSpec((kv_block, tok.lp, HEAD_DIM), kv_idx),
        ],
        out_specs=pl.BlockSpec((tok.lp, 2 * HEAD_DIM), lambda b, p: (b, p)),
        compiler_params=_cparams("arbitrary", "arbitrary"),
        name="prompt_attn",
    )(q, k, v)


def _nb_attn_kernel(q_ref, k_ref, v_ref, ck_ref, cv_ref, bias_ref, o_ref, *, rows):
    band = pl.program_id(2)
    row0 = jnp.clip(NB_BAND * band - NB_ROWS // 2, 0, rows - NB_KROWS)
    start = pl.multiple_of(row0 * GRID_W, (NB_ROWS // 2) * GRID_W)
    n_keys = NB_KROWS * GRID_W
    outs = []
    for hh in range(2):
        kb = k_ref[hh, pl.ds(start, n_keys), :]
        vb = v_ref[hh, pl.ds(start, n_keys), :]
        outs.append(_attend(q_ref[hh], [(kb, vb, bias_ref[hh]), (ck_ref[hh], cv_ref[hh], None)]))
    o_ref[...] = jnp.concatenate(outs, axis=-1).astype(BF16)


def _nb_attn(tok, q, k, v, ck, cv, bias):
    rows = tok.ls // GRID_W
    n_bands = rows // NB_BAND
    tq = NB_BAND * GRID_W
    q0 = tok.tp // tq
    kv0 = tok.tp // tok.ls
    assert tok.tp % tok.ls == 0 and tok.tp % tq == 0
    past = ck.shape[2]

    def bias_idx(b, p, band):
        ty = jnp.where(band == 0, 0, jnp.where(band == n_bands - 1, 2, 1))
        return (ty, p, 0, 0)

    return pl.pallas_call(
        functools.partial(_nb_attn_kernel, rows=rows),
        out_shape=jax.ShapeDtypeStruct((tok.ts, NB_HEADS * HEAD_DIM), BF16),
        grid=(tok.bs, NB_HEADS // 2, n_bands),
        in_specs=[
            pl.BlockSpec((2, tq, HEAD_DIM), lambda b, p, band: (p, q0 + b * n_bands + band, 0)),
            pl.BlockSpec((2, tok.ls, HEAD_DIM), lambda b, p, band: (p, kv0 + b, 0)),
            pl.BlockSpec((2, tok.ls, HEAD_DIM), lambda b, p, band: (p, kv0 + b, 0)),
            pl.BlockSpec((None, 2, past, HEAD_DIM), lambda b, p, band: (b, p, 0, 0)),
            pl.BlockSpec((None, 2, past, HEAD_DIM), lambda b, p, band: (b, p, 0, 0)),
            pl.BlockSpec((None, 2, tq, NB_KROWS * GRID_W), bias_idx),
        ],
        out_specs=pl.BlockSpec((tq, 2 * HEAD_DIM), lambda b, p, band: (b * n_bands + band, p)),
        compiler_params=_cparams("arbitrary", "arbitrary", "arbitrary"),
        name="nb_attn",
    )(q, k, v, ck, cv, bias)


def _nb_bias_tables(rpb, rows):
    n_bands = rows // NB_BAND
    cq = np.arange(GRID_W)
    cstart = np.clip(cq - NB_COLS // 2, 0, GRID_W - NB_COLS)
    col_valid = (cq[None, :] >= cstart[:, None]) & (cq[None, :] < cstart[:, None] + NB_COLS)
    col_idx = np.clip(cq[None, :] - cq[:, None] + NB_COLS - 1, 0, 2 * NB_COLS - 2)
    blocks = jnp.where(col_valid[None, None], rpb[:, :, col_idx], NEG_INF)
    neg = jnp.full((rpb.shape[0], 1, GRID_W, GRID_W), NEG_INF, F32)
    blocks = jnp.concatenate([blocks, neg], axis=1)
    sel = np.zeros((3, NB_BAND, NB_KROWS), np.int32)
    for ty, band in enumerate((0, 1, n_bands - 1)):
        row0 = int(np.clip(NB_BAND * band - NB_ROWS // 2, 0, rows - NB_KROWS))
        for jq in range(NB_BAND):
            r = NB_BAND * band + jq
            rs = int(np.clip(r - NB_ROWS // 2, 0, rows - NB_ROWS))
            for m in range(NB_KROWS):
                rk = row0 + m
                sel[ty, jq, m] = (rk - r + NB_ROWS - 1) if rs <= rk < rs + NB_ROWS else 2 * NB_ROWS - 1
    t = jnp.take(blocks, jnp.asarray(sel.reshape(-1)), axis=1)
    t = t.reshape(rpb.shape[0], 3, NB_BAND, NB_KROWS, GRID_W, GRID_W)
    t = jnp.transpose(t, (1, 0, 2, 4, 3, 5))
    return t.reshape(3, rpb.shape[0], NB_BAND * GRID_W, NB_KROWS * GRID_W)


def _gqa_attn_kernel(q_ref, k_ref, v_ref, ck_ref, cv_ref, o_ref):
    outs = []
    for hh in range(2):
        outs.append(_attend(q_ref[hh], [(ck_ref[0], cv_ref[0], None), (k_ref[0], v_ref[0], None)]))
    o_ref[...] = jnp.concatenate(outs, axis=-1).astype(BF16)


def _gqa_attn(tok, q, k, v, ck, cv):
    group = C_Q_HEADS // C_KV_HEADS
    n_qb = tok.ls // TQ
    q0 = tok.tp // TQ
    kv0 = tok.tp // tok.ls
    past = ck.shape[2]
    return pl.pallas_call(
        _gqa_attn_kernel,
        out_shape=jax.ShapeDtypeStruct((tok.ts, C_Q_HEADS * HEAD_DIM), BF16),
        grid=(tok.bs, C_Q_HEADS // 2, n_qb),
        in_specs=[
            pl.BlockSpec((2, TQ, HEAD_DIM), lambda b, p, i: (p, q0 + b * n_qb + i, 0)),
            pl.BlockSpec((1, tok.ls, HEAD_DIM), lambda b, p, i: (2 * p // group, kv0 + b, 0)),
            pl.BlockSpec((1, tok.ls, HEAD_DIM), lambda b, p, i: (2 * p // group, kv0 + b, 0)),
            pl.BlockSpec((None, 1, past, HEAD_DIM), lambda b, p, i: (b, 2 * p // group, 0, 0)),
            pl.BlockSpec((None, 1, past, HEAD_DIM), lambda b, p, i: (b, 2 * p // group, 0, 0)),
        ],
        out_specs=pl.BlockSpec((TQ, 2 * HEAD_DIM), lambda b, p, i: (b * n_qb + i, p)),
        compiler_params=_cparams("arbitrary", "arbitrary", "arbitrary"),
        name="gqa_attn",
    )(q, k, v, ck, cv)


def _partner(x, s, lane):
    return jnp.where((lane & s) != 0, pltpu.roll(x, s, 1), pltpu.roll(x, LANES - s, 1))


def _head_rms(x, g, lane):
    y = x * x
    for s in (32, 16, 8, 4, 2, 1):
        y = y + _partner(y, s, lane)
    return x * lax.rsqrt(y * (1.0 / HEAD_DIM) + EPS) * g


def _rope(x, cos, sin, lane):
    return x * cos + _partner(x, HEAD_DIM // 4, lane) * sin


def _proj_c_kernel(x_ref, mod_ref, g_ref, w_ref, gq_ref, gk_ref, cos_ref, sin_ref,
                   q_ref, k_ref, v_ref, kf_ref, vf_ref, *, npt):
    i = pl.program_id(0)
    h = _norm_mod(x_ref[...], g_ref[...], mod_ref[3:4, :], mod_ref[4:5, :]).astype(BF16)
    proj = _dot(h, w_ref[...])
    nq = C_Q_HEADS * HEAD_DIM
    nk = C_KV_HEADS * HEAD_DIM
    lane = lax.broadcasted_iota(jnp.int32, (TM, LANES), 1)
    cos = cos_ref[...]
    sin = sin_ref[...]
    gq = gq_ref[...]
    gk = gk_ref[...]
    for c in range(nq // LANES):
        y = _head_rms(proj[:, c * LANES:(c + 1) * LANES], gq, lane)
        y = _rope(y, cos, sin, lane) * Q_SCALE
        q_ref[2 * c] = y[:, :HEAD_DIM].astype(BF16)
        q_ref[2 * c + 1] = y[:, HEAD_DIM:].astype(BF16)
    for c in range(nk // LANES):
        yk = _head_rms(proj[:, nq + c * LANES:nq + (c + 1) * LANES], gk, lane)

        @pl.when(i < npt)
        def _(yk=yk, c=c):
            kf_ref[:, c * LANES:(c + 1) * LANES] = yk

        yr = _rope(yk, cos, sin, lane)
        k_ref[2 * c] = yr[:, :HEAD_DIM].astype(BF16)
        k_ref[2 * c + 1] = yr[:, HEAD_DIM:].astype(BF16)
    for hh in range(C_KV_HEADS):
        v_ref[hh] = proj[:, nq + nk + hh * HEAD_DIM:nq + nk + (hh + 1) * HEAD_DIM].astype(BF16)

    @pl.when(i < npt)
    def _():
        vf_ref[...] = proj[:, nq + nk:nq + 2 * nk]


def _rope_tables(tok):
    n = HEAD_DIM // 4
    inv = ROPE_THETA ** (-jnp.arange(n, dtype=F32) / n)
    t = jnp.arange(tok.ls)
    pos = jnp.stack([(t // GRID_W).astype(F32), (t % GRID_W).astype(F32)], axis=1)
    ang = pos[:, :, None] * inv[None, None, :]
    cos = jnp.cos(ang)
    sin = jnp.sin(ang)
    cos_h = jnp.concatenate([cos, cos], axis=-1).reshape(tok.ls, HEAD_DIM)
    sin_h = jnp.concatenate([-sin, sin], axis=-1).reshape(tok.ls, HEAD_DIM)
    cos_t = jnp.concatenate([jnp.ones((TM, HEAD_DIM), F32), cos_h], axis=0)
    sin_t = jnp.concatenate([jnp.zeros((TM, HEAD_DIM), F32), sin_h], axis=0)
    return jnp.tile(cos_t, (1, 2)), jnp.tile(sin_t, (1, 2))


def _proj_c(tok, x, mod_l, g, w, g_q, g_k, cos_t, sin_t):
    nk = C_KV_HEADS * HEAD_DIM

    def rope_idx(i):
        return (jnp.where(i < tok.npt, 0, 1 + (i - tok.npt) % tok.tiles_per_seq), 0)

    def head_out(n):
        return (pl.BlockSpec((n, TM, HEAD_DIM), lambda i: (0, i, 0)),
                jax.ShapeDtypeStruct((n, tok.t, HEAD_DIM), BF16))

    (qs, qsh), (ks, ksh), (vs, vsh) = head_out(C_Q_HEADS), head_out(C_KV_HEADS), head_out(C_KV_HEADS)
    vec = pl.BlockSpec((1, LANES), lambda i: (0, 0))
    return pl.pallas_call(
        functools.partial(_proj_c_kernel, npt=tok.npt),
        out_shape=(qsh, ksh, vsh, jax.ShapeDtypeStruct((tok.tp, nk), F32), jax.ShapeDtypeStruct((tok.tp, nk), F32)),
        grid=(tok.nt,),
        in_specs=[tok.x_spec(), tok.mod_spec(), pl.BlockSpec((1, D_MODEL), lambda i: (0, 0)),
                  _resident(w.shape, lambda i: (0, 0)), vec, vec,
                  pl.BlockSpec((TM, LANES), rope_idx), pl.BlockSpec((TM, LANES), rope_idx)],
        out_specs=(qs, ks, vs, tok.prompt_spec(nk), tok.prompt_spec(nk)),
        compiler_params=_cparams("arbitrary"),
        name="proj_c",
    )(x, mod_l, g.reshape(1, D_MODEL), w, jnp.tile(g_q, 2).reshape(1, LANES), jnp.tile(g_k, 2).reshape(1, LANES),
      cos_t, sin_t)


def _out_kernel(x_ref, mod_ref, *rest, npt, with_pool):
    if with_pool:
        a_ref, bp_ref, bs_ref, w_ref, o_ref = rest
    else:
        bp_ref, bs_ref, w_ref, o_ref = rest
    i = pl.program_id(0)

    def body(b_ref):
        if with_pool:
            mix = _dot(a_ref[...], w_ref[:A_WIDTH, :]) + _dot(b_ref[...], w_ref[A_WIDTH:, :])
        else:
            mix = _dot(b_ref[...], w_ref[...])
        o_ref[...] = x_ref[...] + mod_ref[5:6, :] * mix

    pl.when(i < npt)(lambda: body(bp_ref))
    pl.when(i >= npt)(lambda: body(bs_ref))


def _mix_out(tok, x, mod_l, a, b_prompt, b_sample, w):
    with_pool = a is not None
    bw = b_prompt.shape[1]
    in_specs = [tok.x_spec(), tok.mod_spec()]
    args = [x, mod_l]
    if with_pool:
        in_specs.append(tok.x_spec(A_WIDTH))
        args.append(a)
    in_specs += [tok.prompt_spec(bw), tok.sample_spec(bw), _resident(w.shape, lambda i: (0, 0))]
    args += [b_prompt, b_sample, w]
    return pl.pallas_call(
        functools.partial(_out_kernel, npt=tok.npt, with_pool=with_pool),
        out_shape=jax.ShapeDtypeStruct((tok.t, D_MODEL), F32),
        grid=(tok.nt,),
        in_specs=in_specs,
        out_specs=tok.x_spec(),
        compiler_params=_cparams("arbitrary"),
        name="mix_out_ab" if with_pool else "mix_out_c",
    )(*args)


def _ctx_heads(cache):
    return jnp.transpose(cache, (0, 2, 1, 3)).astype(BF16)


def kernel(x_prompt, x_sample, cache_nb_k, cache_nb_v, cache_attn_k, cache_attn_v, c, c_ctx, w_mod, b_mod, g_norm, w_ffn_in, w_ffn_out, w_in_ab, w_pool, pool_scale, nb_rpb, w_out_ab, w_qkv_c, g_qnorm, g_knorm, w_out_c, g_final):
    bp, lp, d = x_prompt.shape
    bs, ls, _ = x_sample.shape
    assert d == D_MODEL and w_ffn_in.shape[-1] == 2 * D_FF and w_mod.shape[0] == DEPTH
    tok = _Tokens(bp, lp, bs, ls)

    x = jnp.concatenate([x_prompt.reshape(tok.tp, d), x_sample.reshape(tok.ts, d)], axis=0)
    n_rows = SUBLANES * (-(-(1 + bs) // SUBLANES))
    cvec = jnp.concatenate([c_ctx[None], c, jnp.zeros((n_rows - 1 - bs, d), F32)], axis=0)
    mod = _ada(cvec, w_mod, b_mod).reshape(DEPTH, n_rows, N_MOD, d)

    w_ffn_in_b = w_ffn_in.astype(BF16)
    w_ffn_out_b = w_ffn_out.astype(BF16)
    w_in_ab_b = w_in_ab.astype(BF16)
    w_pool_b = w_pool.astype(BF16)
    w_out_ab_b = w_out_ab.astype(BF16)
    w_qkv_c_b = w_qkv_c.astype(BF16)
    w_out_c_b = w_out_c.astype(BF16)
    cos_t, sin_t = _rope_tables(tok)

    nb_k, nb_v, at_k, at_v = [], [], [], []
    for l in range(DEPTH):
        x = _ffn(tok, x, mod[l], g_norm[l, 0], w_ffn_in_b[l, 0], w_ffn_out_b[l, 0], sub=0)
        if l % 2 == 0:
            e = l // 2
            u, q, k, v, kf, vf = _proj_ab(tok, x, mod[l], g_norm[l, 1], w_in_ab_b[e])
            a = _pool(tok, u, w_pool_b[e], pool_scale[e])
            b_p = _prompt_attn(tok, q, k, v, NB_HEADS, NB_HEADS)
            bias = _nb_bias_tables(nb_rpb[e], ls // GRID_W)
            b_s = _nb_attn(tok, q, k, v, _ctx_heads(cache_nb_k[:, e]), _ctx_heads(cache_nb_v[:, e]), bias)
            x = _mix_out(tok, x, mod[l], a, b_p, b_s, w_out_ab_b[e])
            nb_k.append(kf.reshape(bp, lp, NB_HEADS, HEAD_DIM))
            nb_v.append(vf.reshape(bp, lp, NB_HEADS, HEAD_DIM))
        else:
            o = l // 2
            q, k, v, kf, vf = _proj_c(tok, x, mod[l], g_norm[l, 1], w_qkv_c_b[o], g_qnorm[o], g_knorm[o], cos_t, sin_t)
            b_p = _prompt_attn(tok, q, k, v, C_Q_HEADS, C_KV_HEADS)
            b_s = _gqa_attn(tok, q, k, v, _ctx_heads(cache_attn_k[:, o]), _ctx_heads(cache_attn_v[:, o]))
            x = _mix_out(tok, x, mod[l], None, b_p, b_s, w_out_c_b[o])
            at_k.append(kf.reshape(bp, lp, C_KV_HEADS, HEAD_DIM))
            at_v.append(vf.reshape(bp, lp, C_KV_HEADS, HEAD_DIM))
        x = _ffn(tok, x, mod[l], g_norm[l, 2], w_ffn_in_b[l, 1], w_ffn_out_b[l, 1], sub=2,
                 g_final=g_final if l == DEPTH - 1 else None)

    y_prompt = x[:tok.tp].reshape(bp, lp, d)
    y_sample = x[tok.tp:].reshape(bs, ls, d)
    return (y_prompt, y_sample, jnp.stack(nb_k, axis=1), jnp.stack(nb_v, axis=1),
            jnp.stack(at_k, axis=1), jnp.stack(at_v, axis=1))
```

```python
import functools
import math

import numpy as np
import jax
import jax.numpy as jnp
from jax import lax
from jax.experimental import pallas as pl
from jax.experimental.pallas import tpu as pltpu

F32 = jnp.float32
BF16 = jnp.bfloat16

D_MODEL = 1024
DEPTH = 4
GRID_W = 64
D_FF = 2816
N_MOD = 9
A_WIDTH = 512
POOL_WINDOWS = (2, 4, 8, 16)
POOL_GROUP = 128
HEAD_DIM = 64
NB_HEADS = 8
NB_ROWS = 8
NB_COLS = 16
C_Q_HEADS = 16
C_KV_HEADS = 4
ROPE_THETA = 10000.0
EPS = 1e-6
NEG_INF = -1e30
LOG2E = math.log2(math.e)
Q_SCALE = HEAD_DIM ** -0.5 * LOG2E

LANES = 128
SUBLANES = 8
VMEM_LIMIT_BYTES = 56 * 1024 * 1024

TM = 512
TF = 256
TPOOL = 256
POOL_HALO = 8
NB_BAND = 8
NB_KROWS = 16
GQ_TOK = 128
GQ_UNROLL = 4
KCHUNK = 256


def _cparams(*sem):
    return pltpu.CompilerParams(dimension_semantics=sem, vmem_limit_bytes=VMEM_LIMIT_BYTES)


def _resident(shape, index_map):
    return pl.BlockSpec(shape, index_map, pipeline_mode=pl.Buffered(1))


def _dot(a, b):
    return jnp.dot(a, b, preferred_element_type=F32)


def _dot_nt(a, b):
    return lax.dot_general(a, b, (((1,), (1,)), ((), ())), preferred_element_type=F32)


def _norm_mod(x, g, shift, scale):
    ms = jnp.mean(x * x, axis=-1, keepdims=True)
    y = x * lax.rsqrt(ms + EPS) * g
    return y * (1.0 + scale) + shift


def _with_ones(v):
    return jnp.concatenate([v, jnp.ones_like(v)], axis=-1).astype(BF16)


def _ada_kernel(c_ref, w_ref, b_ref, o_ref):
    c = c_ref[...]
    s = (c * jax.nn.sigmoid(c)).astype(BF16)
    o_ref[...] = _dot(s, w_ref[...].astype(BF16)) + b_ref[...]


def _ada(cvec, w_mod, b_mod):
    rows = cvec.shape[0]
    n = w_mod.shape[-1]
    tn = D_MODEL
    return pl.pallas_call(
        _ada_kernel,
        out_shape=jax.ShapeDtypeStruct((DEPTH, rows, n), F32),
        grid=(DEPTH, n // tn),
        in_specs=[
            pl.BlockSpec((rows, D_MODEL), lambda l, j: (0, 0)),
            pl.BlockSpec((None, D_MODEL, tn), lambda l, j: (l, 0, j)),
            pl.BlockSpec((None, 1, tn), lambda l, j: (l, 0, j)),
        ],
        out_specs=pl.BlockSpec((None, rows, tn), lambda l, j: (l, 0, j)),
        compiler_params=_cparams("arbitrary", "arbitrary"),
        name="ada_mod",
    )(cvec, w_mod, b_mod.reshape(DEPTH, 1, n))


class _Tokens:
    def __init__(self, bp, lp, bs, ls):
        self.bp, self.lp, self.bs, self.ls = bp, lp, bs, ls
        self.tp = bp * lp
        self.ts = bs * ls
        self.t = self.tp + self.ts
        assert self.tp % TM == 0 and ls % TM == 0 and lp % TPOOL == 0 and ls % TPOOL == 0
        self.npt = self.tp // TM
        self.nt = self.t // TM
        self.tiles_per_seq = ls // TM

    def mod_row(self, i):
        return jnp.where(i < self.npt, 0, 1 + (i - self.npt) // self.tiles_per_seq)

    def x_spec(self, width=D_MODEL):
        return pl.BlockSpec((TM, width), lambda i: (i, 0))

    def mod_spec(self, layer):
        return pl.BlockSpec((None, None, N_MOD, D_MODEL), lambda i: (layer, self.mod_row(i), 0, 0))

    def prompt_spec(self, width):
        return pl.BlockSpec((TM, width), lambda i: (jnp.minimum(i, self.npt - 1), 0))

    def sample_spec(self, width):
        return pl.BlockSpec((TM, width), lambda i: (jnp.maximum(i - self.npt, 0), 0))


def _vec_spec(layer, sub):
    return pl.BlockSpec((None, None, 1, D_MODEL), lambda i: (layer, sub, 0, 0))


def _ffn_kernel(*refs, sub, npt, split_in, final):
    refs = list(refs)
    if split_in:
        xp_ref, xs_ref = refs[:2]
        refs = refs[2:]
    else:
        x_ref = refs.pop(0)
    mod_ref, g_ref, win_ref, wout_ref = refs[:4]
    refs = refs[4:]
    if final:
        gf_ref, yp_ref, ys_ref, act_ref = refs
    else:
        o_ref, act_ref = refs
    i = pl.program_id(0)
    if split_in:
        x = jnp.where(i < npt, xp_ref[...], xs_ref[...])
    else:
        x = x_ref[...]
    h = _norm_mod(x, g_ref[...], mod_ref[3 * sub:3 * sub + 1, :], mod_ref[3 * sub + 1:3 * sub + 2, :]).astype(BF16)
    for j in range(D_FF // TF):
        a = _dot(h, win_ref[:, j * TF:(j + 1) * TF])
        b = _dot(h, win_ref[:, D_FF + j * TF:D_FF + (j + 1) * TF])
        act_ref[:, j * TF:(j + 1) * TF] = (a * jax.nn.sigmoid(a) * b).astype(BF16)
    y = x + mod_ref[3 * sub + 2:3 * sub + 3, :] * (0.5 * _dot(act_ref[...], wout_ref[...]))
    if final:
        ms = jnp.mean(y * y, axis=-1, keepdims=True)
        y = y * lax.rsqrt(ms + EPS) * gf_ref[...]

        @pl.when(i < npt)
        def _():
            yp_ref[...] = y

        @pl.when(i >= npt)
        def _():
            ys_ref[...] = y
    else:
        o_ref[...] = y


def _ffn(tok, x, mod, g_norm, w_in, w_out, layer, sub, g_final=None):
    split_in = isinstance(x, tuple)
    final = g_final is not None
    which = sub // 2
    if split_in:
        in_specs = [tok.prompt_spec(D_MODEL), tok.sample_spec(D_MODEL)]
        args = list(x)
    else:
        in_specs = [tok.x_spec()]
        args = [x]
    in_specs += [
        tok.mod_spec(layer), _vec_spec(layer, sub),
        _resident((None, None, D_MODEL, 2 * D_FF), lambda i: (layer, which, 0, 0)),
        _resident((None, None, D_FF, D_MODEL), lambda i: (layer, which, 0, 0)),
    ]
    args += [mod, g_norm, w_in, w_out]
    if final:
        in_specs.append(pl.BlockSpec((1, D_MODEL), lambda i: (0, 0)))
        args.append(g_final.reshape(1, D_MODEL))
        out_shape = (jax.ShapeDtypeStruct((tok.tp, D_MODEL), F32), jax.ShapeDtypeStruct((tok.ts, D_MODEL), F32))
        out_specs = (tok.prompt_spec(D_MODEL), tok.sample_spec(D_MODEL))
    else:
        out_shape = jax.ShapeDtypeStruct((tok.t, D_MODEL), F32)
        out_specs = tok.x_spec()
    return pl.pallas_call(
        functools.partial(_ffn_kernel, sub=sub, npt=tok.npt, split_in=split_in, final=final),
        out_shape=out_shape,
        grid=(tok.nt,),
        in_specs=in_specs,
        out_specs=out_specs,
        scratch_shapes=[pltpu.VMEM((TM, D_FF), BF16)],
        compiler_params=_cparams("arbitrary"),
        name="ffn_final" if final else ("ffn_first" if split_in else "ffn"),
    )(*args)


def _proj_ab_kernel(x_ref, mod_ref, g_ref, w_ref, u_ref, q_ref, k_ref, v_ref, kf_ref, vf_ref, *, npt):
    i = pl.program_id(0)
    h = _norm_mod(x_ref[...], g_ref[...], mod_ref[3:4, :], mod_ref[4:5, :]).astype(BF16)
    proj = _dot(h, w_ref[...])
    u_ref[...] = proj[:, :A_WIDTH]
    hw = NB_HEADS * HEAD_DIM
    for hh in range(NB_HEADS):
        lo = A_WIDTH + hh * HEAD_DIM
        q_ref[hh] = (proj[:, lo:lo + HEAD_DIM] * Q_SCALE).astype(BF16)
        k_ref[hh] = proj[:, lo + hw:lo + hw + HEAD_DIM].astype(BF16)
        v_ref[hh] = _with_ones(proj[:, lo + 2 * hw:lo + 2 * hw + HEAD_DIM])

    @pl.when(i < npt)
    def _():
        kf_ref[...] = proj[:, A_WIDTH + hw:A_WIDTH + 2 * hw]
        vf_ref[...] = proj[:, A_WIDTH + 2 * hw:A_WIDTH + 3 * hw]


def _head_major(n_heads, tok, width=HEAD_DIM):
    return (pl.BlockSpec((n_heads, TM, width), lambda i: (0, i, 0)),
            jax.ShapeDtypeStruct((n_heads, tok.t, width), BF16))


def _proj_ab(tok, x, mod, g_norm, w, layer):
    e = layer // 2
    hw = NB_HEADS * HEAD_DIM
    (qs, qsh), (ks, ksh), (vs, vsh) = (_head_major(NB_HEADS, tok), _head_major(NB_HEADS, tok),
                                       _head_major(NB_HEADS, tok, 2 * HEAD_DIM))
    return pl.pallas_call(
        functools.partial(_proj_ab_kernel, npt=tok.npt),
        out_shape=(jax.ShapeDtypeStruct((tok.t, A_WIDTH), F32), qsh, ksh, vsh,
                   jax.ShapeDtypeStruct((tok.tp, hw), F32), jax.ShapeDtypeStruct((tok.tp, hw), F32)),
        grid=(tok.nt,),
        in_specs=[tok.x_spec(), tok.mod_spec(layer), _vec_spec(layer, 1),
                  _resident((None,) + w.shape[1:], lambda i: (e, 0, 0))],
        out_specs=(tok.x_spec(A_WIDTH), qs, ks, vs, tok.prompt_spec(hw), tok.prompt_spec(hw)),
        compiler_params=_cparams("arbitrary"),
        name="proj_ab",
    )(x, mod, g_norm, w)


def _pool_kernel(u_ref, prev_ref, next_ref, w_ref, sc_ref, o_ref, *, n_prompt_tiles, lp, ls):
    i = pl.program_id(0)
    is_p = i < n_prompt_tiles
    tiles_s = ls // TPOOL
    j = jnp.where(is_p, 0, (i - n_prompt_tiles) % tiles_s)
    seq_len = jnp.where(is_p, lp, ls)
    pos0 = j * TPOOL
    first = j == 0
    last = pos0 + TPOOL == seq_len
    prev = jnp.where(first, 0.0, prev_ref[...])
    nxt = jnp.where(last, 0.0, next_ref[...])
    ext = jnp.concatenate([prev, u_ref[...], nxt], axis=0)
    n_ext = TPOOL + 2 * POOL_HALO
    pos = pos0 + lax.broadcasted_iota(jnp.int32, (TPOOL, 1), 0)

    def fwd(v, d):
        return pltpu.roll(v, n_ext - d, 0)

    for gi, w in enumerate(POOL_WINDOWS):
        e = ext[:, gi * POOL_GROUP:(gi + 1) * POOL_GROUP]
        acc = e
        span = 1
        while span < w:
            acc = acc + fwd(acc, span)
            span *= 2
        half = w // 2
        centred = pltpu.roll(acc, half, 0)
        s = centred[POOL_HALO:POOL_HALO + TPOOL]
        cnt = (jnp.minimum(pos + (w - 1 - half), seq_len - 1) - jnp.maximum(pos - half, 0) + 1).astype(F32)
        pooled = (s / cnt - e[POOL_HALO:POOL_HALO + TPOOL]).astype(BF16)
        mixed = _dot(pooled, w_ref[gi])
        o_ref[:, gi * POOL_GROUP:(gi + 1) * POOL_GROUP] = (
            mixed * sc_ref[:, gi * POOL_GROUP:(gi + 1) * POOL_GROUP]).astype(BF16)


def _pool(tok, u, w_pool, pool_scale, e):
    n_tiles = tok.t // TPOOL
    per = TPOOL // POOL_HALO
    n_halo_blocks = tok.t // POOL_HALO
    return pl.pallas_call(
        functools.partial(_pool_kernel, n_prompt_tiles=tok.tp // TPOOL, lp=tok.lp, ls=tok.ls),
        out_shape=jax.ShapeDtypeStruct((tok.t, A_WIDTH), BF16),
        grid=(n_tiles,),
        in_specs=[
            pl.BlockSpec((TPOOL, A_WIDTH), lambda i: (i, 0)),
            pl.BlockSpec((POOL_HALO, A_WIDTH), lambda i: (jnp.maximum(i * per - 1, 0), 0)),
            pl.BlockSpec((POOL_HALO, A_WIDTH), lambda i: (jnp.minimum((i + 1) * per, n_halo_blocks - 1), 0)),
            pl.BlockSpec((None, len(POOL_WINDOWS), POOL_GROUP, POOL_GROUP), lambda i: (e, 0, 0, 0)),
            pl.BlockSpec((None, 1, A_WIDTH), lambda i: (e, 0, 0)),
        ],
        out_specs=pl.BlockSpec((TPOOL, A_WIDTH), lambda i: (i, 0)),
        compiler_params=_cparams("arbitrary"),
        name="pool_mix",
    )(u, u, u, w_pool, pool_scale.reshape(pool_scale.shape[0], 1, A_WIDTH))


def _normalise(pv):
    den = pltpu.roll(pv, HEAD_DIM, 1)
    return (pv / den)[:, :HEAD_DIM]


def _attend(q, sources):
    chunks = []
    mx = None
    for k, v1, bias in sources:
        for c0 in range(0, k.shape[0], KCHUNK):
            s = _dot_nt(q, k[c0:c0 + KCHUNK, :])
            if bias is not None:
                s = s + bias(c0)
            part = jnp.maximum(s[:, :LANES], s[:, LANES:])
            mx = part if mx is None else jnp.maximum(mx, part)
            chunks.append((s, v1, c0))
    m = jnp.broadcast_to(mx.max(axis=-1, keepdims=True), mx.shape)
    pv = None
    for s, v1, c0 in chunks:
        p = jnp.concatenate([jnp.exp2(s[:, h * LANES:(h + 1) * LANES] - m) for h in range(KCHUNK // LANES)],
                            axis=-1).astype(BF16)
        t = _dot(p, v1[c0:c0 + KCHUNK, :])
        pv = t if pv is None else pv + t
    return _normalise(pv)


def _attn_self_kernel(q_ref, k_ref, v_ref, o_ref, *, shared_kv):
    outs = []
    for hh in range(2):
        kv = 0 if shared_kv else hh
        outs.append(_attend(q_ref[hh], [(k_ref.at[kv], v_ref.at[kv], None)]))
    o_ref[...] = jnp.concatenate(outs, axis=-1).astype(BF16)


def _prompt_attn(tok, q, k, v, n_q_heads, n_kv_heads):
    group = n_q_heads // n_kv_heads
    shared = group > 1
    kv_block = 1 if shared else 2
    kv_idx = (lambda b, p: (2 * p // group, b, 0)) if shared else (lambda b, p: (p, b, 0))
    return pl.pallas_call(
        functools.partial(_attn_self_kernel, shared_kv=shared),
        out_shape=jax.ShapeDtypeStruct((tok.tp, n_q_heads * HEAD_DIM), BF16),
        grid=(tok.bp, n_q_heads // 2),
        in_specs=[
            pl.BlockSpec((2, tok.lp, HEAD_DIM), lambda b, p: (p, b, 0)),
            pl.BlockSpec((kv_block, tok.lp, HEAD_DIM), kv_idx),
            pl.BlockSpec((kv_block, tok.lp, 2 * HEAD_DIM), kv_idx),
        ],
        out_specs=pl.BlockSpec((tok.lp, 2 * HEAD_DIM), lambda b, p: (b, p)),
        compiler_params=_cparams("arbitrary", "arbitrary"),
        name="prompt_attn",
    )(q, k, v)


def _nb_attn_kernel(q_ref, k_ref, v_ref, ck_ref, cv_ref, rel_ref, rowmask_ref, o_ref, *, rows):
    band = pl.program_id(2)
    row0 = jnp.clip(NB_BAND * band - NB_ROWS // 2, 0, rows - NB_KROWS)
    start = pl.multiple_of(row0 * GRID_W, (NB_ROWS // 2) * GRID_W)
    n_keys = NB_KROWS * GRID_W
    outs = []
    for hh in range(2):
        def bias(c0, hh=hh):
            strips = []
            for j in range(NB_BAND):
                off = (NB_BAND - 1 - j) * GRID_W
                copy = (off // GRID_W) % 2
                lo = off - copy * GRID_W + c0
                strips.append(rel_ref[hh, copy, :, lo:lo + KCHUNK])
            return jnp.concatenate(strips, axis=0) + rowmask_ref[:, c0:c0 + KCHUNK]

        kb = k_ref.at[hh, pl.ds(start, n_keys), :]
        vb = v_ref.at[hh, pl.ds(start, n_keys), :]
        outs.append(_attend(q_ref[hh], [(kb, vb, bias), (ck_ref.at[hh], cv_ref.at[hh], None)]))
    o_ref[...] = jnp.concatenate(outs, axis=-1).astype(BF16)


def _nb_attn(tok, q, k, v, ck, cv, rel, rowmask):
    rows = tok.ls // GRID_W
    n_bands = rows // NB_BAND
    tq = NB_BAND * GRID_W
    q0 = tok.tp // tq
    kv0 = tok.tp // tok.ls
    assert tok.tp % tok.ls == 0 and tok.tp % tq == 0
    past = ck.shape[2]

    def band_type(band):
        return jnp.where(band == 0, 0, jnp.where(band == n_bands - 1, 2, 1))

    return pl.pallas_call(
        functools.partial(_nb_attn_kernel, rows=rows),
        out_shape=jax.ShapeDtypeStruct((tok.ts, NB_HEADS * HEAD_DIM), BF16),
        grid=(tok.bs, NB_HEADS // 2, n_bands),
        in_specs=[
            pl.BlockSpec((2, tq, HEAD_DIM), lambda b, p, band: (p, q0 + b * n_bands + band, 0)),
            pl.BlockSpec((2, tok.ls, HEAD_DIM), lambda b, p, band: (p, kv0 + b, 0)),
            pl.BlockSpec((2, tok.ls, 2 * HEAD_DIM), lambda b, p, band: (p, kv0 + b, 0)),
            pl.BlockSpec((None, 2, past, HEAD_DIM), lambda b, p, band: (b, p, 0, 0)),
            pl.BlockSpec((None, 2, past, 2 * HEAD_DIM), lambda b, p, band: (b, p, 0, 0)),
            pl.BlockSpec((None, 2, 2, GRID_W, rel.shape[-1]), lambda b, p, band: (band_type(band), p, 0, 0, 0)),
            pl.BlockSpec((None, tq, NB_KROWS * GRID_W), lambda b, p, band: (band_type(band), 0, 0)),
        ],
        out_specs=pl.BlockSpec((tq, 2 * HEAD_DIM), lambda b, p, band: (b * n_bands + band, p)),
        compiler_params=_cparams("arbitrary", "arbitrary", "arbitrary"),
        name="nb_attn",
    )(q, k, v, ck, cv, rel, rowmask)


def _nb_band_geometry(rows):
    n_bands = rows // NB_BAND
    offs = []
    valid = np.zeros((3, NB_BAND, NB_KROWS), bool)
    for ty, band in enumerate((0, 1, n_bands - 1)):
        row0 = int(np.clip(NB_BAND * band - NB_ROWS // 2, 0, rows - NB_KROWS))
        offs.append(row0 - NB_BAND * band + NB_ROWS - 1)
        for jq in range(NB_BAND):
            r = NB_BAND * band + jq
            rs = int(np.clip(r - NB_ROWS // 2, 0, rows - NB_ROWS))
            for m in range(NB_KROWS):
                valid[ty, jq, m] = rs <= row0 + m < rs + NB_ROWS
    return offs, valid


def _nb_bias_tables(rpb, rows):
    n_heads = rpb.shape[0]
    cq = np.arange(GRID_W)
    cstart = np.clip(cq - NB_COLS // 2, 0, GRID_W - NB_COLS)
    col_valid = (cq[None, :] >= cstart[:, None]) & (cq[None, :] < cstart[:, None] + NB_COLS)
    col_idx = np.clip(cq[None, :] - cq[:, None] + NB_COLS - 1, 0, 2 * NB_COLS - 2)
    blocks = jnp.where(col_valid[None, None], rpb[:, :, col_idx] * LOG2E, NEG_INF)
    n_d = 2 * NB_ROWS - 1
    blocks = jnp.concatenate([blocks, jnp.zeros((n_heads, 1, GRID_W, GRID_W), F32)], axis=1)
    offs, valid = _nb_band_geometry(rows)
    n_strip = NB_BAND + NB_KROWS
    rel = []
    for c in offs:
        copies = []
        for copy in range(2):
            d = np.arange(n_strip) + c - (NB_BAND - 1) + copy
            d = np.where((d >= 0) & (d < n_d), d, n_d)
            t = jnp.take(blocks, jnp.asarray(d), axis=1)
            copies.append(jnp.transpose(t, (0, 2, 1, 3)).reshape(n_heads, GRID_W, n_strip * GRID_W))
        rel.append(jnp.stack(copies, axis=1))
    rel = jnp.stack(rel, axis=0)
    small = jnp.where(jnp.asarray(valid), 0.0, NEG_INF).astype(F32)
    rowmask = jnp.broadcast_to(small[:, :, None, :, None], (3, NB_BAND, GRID_W, NB_KROWS, GRID_W))
    return rel, rowmask.reshape(3, NB_BAND * GRID_W, NB_KROWS * GRID_W)


def _gqa_attn_kernel(q_ref, k_ref, v_ref, ck_ref, cv_ref, o_ref, s0_ref, s1_ref, mx_ref, pv_ref,
                     *, group, n_ctx, n_self):
    n_chunks = (n_ctx + n_self) // KCHUNK
    n_ctx_chunks = n_ctx // KCHUNK
    n_blocks = q_ref.shape[1] // GQ_TOK
    assert n_blocks % GQ_UNROLL == 0 and GQ_UNROLL % 2 == 0
    s_refs = (s0_ref, s1_ref)

    def keys(c):
        if c < n_ctx_chunks:
            return ck_ref[0, c * KCHUNK:(c + 1) * KCHUNK, :], cv_ref[0, c * KCHUNK:(c + 1) * KCHUNK, :]
        c -= n_ctx_chunks
        return k_ref[0, c * KCHUNK:(c + 1) * KCHUNK, :], v_ref[0, c * KCHUNK:(c + 1) * KCHUNK, :]

    def scores(qb, slot):
        r0 = pl.multiple_of(qb * GQ_TOK, GQ_TOK)
        q = jnp.concatenate([q_ref[g, pl.ds(r0, GQ_TOK), :] for g in range(group)], axis=0)
        mx = None
        for c in range(n_chunks):
            s = _dot_nt(q, keys(c)[0])
            s_refs[slot][:, c * KCHUNK:(c + 1) * KCHUNK] = s
            part = jnp.maximum(s[:, :LANES], s[:, LANES:])
            mx = part if mx is None else jnp.maximum(mx, part)
        mx_ref[slot] = mx

    def outputs(slot):
        mx = mx_ref[slot]
        m = jnp.broadcast_to(mx.max(axis=-1, keepdims=True), mx.shape)
        pv = None
        for c in range(n_chunks):
            p = jnp.concatenate(
                [jnp.exp2(s_refs[slot][:, c * KCHUNK + h * LANES:c * KCHUNK + (h + 1) * LANES] - m)
                 for h in range(KCHUNK // LANES)], axis=-1).astype(BF16)
            t = _dot(p, keys(c)[1])
            pv = t if pv is None else pv + t
        pv_ref[slot] = pv

    def finalize(qb, slot):
        r0 = pl.multiple_of(qb * GQ_TOK, GQ_TOK)
        o = _normalise(pv_ref[slot])
        o_ref[pl.ds(r0, GQ_TOK), :] = jnp.concatenate(
            [o[g * GQ_TOK:(g + 1) * GQ_TOK] for g in range(group)], axis=-1).astype(BF16)

    pv_ref[...] = jnp.ones(pv_ref.shape, F32)
    scores(0, 0)

    def body(j, carry):
        base = GQ_UNROLL * j
        for u in range(GQ_UNROLL):
            slot = u % 2
            finalize(jnp.maximum(base + u - 2, 0), slot)
            outputs(slot)
            scores(jnp.minimum(base + u + 1, n_blocks - 1), 1 - slot)
        return carry

    lax.fori_loop(0, n_blocks // GQ_UNROLL, body, 0)
    finalize(n_blocks - 2, 0)
    finalize(n_blocks - 1, 1)


def _gqa_attn(tok, q, k, v, ck, cv):
    group = C_Q_HEADS // C_KV_HEADS
    kv0 = tok.tp // tok.ls
    past = ck.shape[2]
    assert past % KCHUNK == 0 and tok.ls % KCHUNK == 0 and tok.ls % GQ_TOK == 0
    return pl.pallas_call(
        functools.partial(_gqa_attn_kernel, group=group, n_ctx=past, n_self=tok.ls),
        out_shape=jax.ShapeDtypeStruct((tok.ts, C_Q_HEADS * HEAD_DIM), BF16),
        grid=(tok.bs, C_KV_HEADS),
        in_specs=[
            pl.BlockSpec((group, tok.ls, HEAD_DIM), lambda b, g: (g, kv0 + b, 0)),
            pl.BlockSpec((1, tok.ls, HEAD_DIM), lambda b, g: (g, kv0 + b, 0)),
            pl.BlockSpec((1, tok.ls, 2 * HEAD_DIM), lambda b, g: (g, kv0 + b, 0)),
            pl.BlockSpec((None, 1, past, HEAD_DIM), lambda b, g: (b, g, 0, 0)),
            pl.BlockSpec((None, 1, past, 2 * HEAD_DIM), lambda b, g: (b, g, 0, 0)),
        ],
        out_specs=pl.BlockSpec((tok.ls, group * HEAD_DIM), lambda b, g: (b, g)),
        scratch_shapes=[pltpu.VMEM((group * GQ_TOK, past + tok.ls), F32),
                        pltpu.VMEM((group * GQ_TOK, past + tok.ls), F32),
                        pltpu.VMEM((2, group * GQ_TOK, LANES), F32),
                        pltpu.VMEM((2, group * GQ_TOK, LANES), F32)],
        compiler_params=_cparams("arbitrary", "arbitrary"),
        name="gqa_attn",
    )(q, k, v, ck, cv)


def _partner(x, s, lane):
    return jnp.where((lane & s) != 0, pltpu.roll(x, s, 1), pltpu.roll(x, LANES - s, 1))


def _head_rms(x, g, seg_ones):
    y = x * x
    hi = y.astype(BF16)
    lo = (y - hi.astype(F32)).astype(BF16)
    ss = _dot(hi, seg_ones) + _dot(lo, seg_ones)
    return x * lax.rsqrt(ss * (1.0 / HEAD_DIM) + EPS) * g


def _rope(x, cos, sin, lane):
    return x * cos + _partner(x, HEAD_DIM // 4, lane) * sin


def _proj_c_kernel(x_ref, mod_ref, g_ref, w_ref, gq_ref, gk_ref, cos_ref, sin_ref,
                   q_ref, k_ref, v_ref, kf_ref, vf_ref, *, npt):
    i = pl.program_id(0)
    h = _norm_mod(x_ref[...], g_ref[...], mod_ref[3:4, :], mod_ref[4:5, :]).astype(BF16)
    proj = _dot(h, w_ref[...])
    nq = C_Q_HEADS * HEAD_DIM
    nk = C_KV_HEADS * HEAD_DIM
    lane = lax.broadcasted_iota(jnp.int32, (TM, LANES), 1)
    seg_ones = (lax.broadcasted_iota(jnp.int32, (LANES, LANES), 0) // HEAD_DIM ==
                lax.broadcasted_iota(jnp.int32, (LANES, LANES), 1) // HEAD_DIM).astype(BF16)
    cos = cos_ref[...]
    sin = sin_ref[...]
    gq = gq_ref[...]
    gk = gk_ref[...]
    for c in range(nq // LANES):
        y = _head_rms(proj[:, c * LANES:(c + 1) * LANES], gq, seg_ones)
        y = _rope(y, cos, sin, lane) * Q_SCALE
        q_ref[2 * c] = y[:, :HEAD_DIM].astype(BF16)
        q_ref[2 * c + 1] = y[:, HEAD_DIM:].astype(BF16)
    for c in range(nk // LANES):
        yk = _head_rms(proj[:, nq + c * LANES:nq + (c + 1) * LANES], gk, seg_ones)

        @pl.when(i < npt)
        def _(yk=yk, c=c):
            kf_ref[:, c * LANES:(c + 1) * LANES] = yk

        yr = _rope(yk, cos, sin, lane)
        k_ref[2 * c] = yr[:, :HEAD_DIM].astype(BF16)
        k_ref[2 * c + 1] = yr[:, HEAD_DIM:].astype(BF16)
    for hh in range(C_KV_HEADS):
        v_ref[hh] = _with_ones(proj[:, nq + nk + hh * HEAD_DIM:nq + nk + (hh + 1) * HEAD_DIM])

    @pl.when(i < npt)
    def _():
        vf_ref[...] = proj[:, nq + nk:nq + 2 * nk]


def _rope_tables(tok):
    n = HEAD_DIM // 4
    inv = ROPE_THETA ** (-jnp.arange(n, dtype=F32) / n)
    t = jnp.arange(tok.ls)
    pos = jnp.stack([(t // GRID_W).astype(F32), (t % GRID_W).astype(F32)], axis=1)
    ang = pos[:, :, None] * inv[None, None, :]
    cos = jnp.cos(ang)
    sin = jnp.sin(ang)
    cos_h = jnp.concatenate([cos, cos], axis=-1).reshape(tok.ls, HEAD_DIM)
    sin_h = jnp.concatenate([-sin, sin], axis=-1).reshape(tok.ls, HEAD_DIM)
    cos_t = jnp.concatenate([jnp.ones((TM, HEAD_DIM), F32), cos_h], axis=0)
    sin_t = jnp.concatenate([jnp.zeros((TM, HEAD_DIM), F32), sin_h], axis=0)
    return jnp.tile(cos_t, (1, 2)), jnp.tile(sin_t, (1, 2))


def _proj_c(tok, x, mod, g_norm, w, g_q, g_k, cos_t, sin_t, layer):
    o = layer // 2
    nk = C_KV_HEADS * HEAD_DIM

    def rope_idx(i):
        return (jnp.where(i < tok.npt, 0, 1 + (i - tok.npt) % tok.tiles_per_seq), 0)

    (qs, qsh), (ks, ksh), (vs, vsh) = (_head_major(C_Q_HEADS, tok), _head_major(C_KV_HEADS, tok),
                                       _head_major(C_KV_HEADS, tok, 2 * HEAD_DIM))
    vec = pl.BlockSpec((None, 1, LANES), lambda i: (o, 0, 0))
    n = g_q.shape[0]
    return pl.pallas_call(
        functools.partial(_proj_c_kernel, npt=tok.npt),
        out_shape=(qsh, ksh, vsh, jax.ShapeDtypeStruct((tok.tp, nk), F32), jax.ShapeDtypeStruct((tok.tp, nk), F32)),
        grid=(tok.nt,),
        in_specs=[tok.x_spec(), tok.mod_spec(layer), _vec_spec(layer, 1),
                  _resident((None,) + w.shape[1:], lambda i: (o, 0, 0)), vec, vec,
                  pl.BlockSpec((TM, LANES), rope_idx), pl.BlockSpec((TM, LANES), rope_idx)],
        out_specs=(qs, ks, vs, tok.prompt_spec(nk), tok.prompt_spec(nk)),
        compiler_params=_cparams("arbitrary"),
        name="proj_c",
    )(x, mod, g_norm, w, jnp.tile(g_q, (1, 2)).reshape(n, 1, LANES), jnp.tile(g_k, (1, 2)).reshape(n, 1, LANES),
      cos_t, sin_t)


def _out_kernel(x_ref, mod_ref, *rest, npt, with_pool):
    if with_pool:
        a_ref, bp_ref, bs_ref, w_ref, o_ref = rest
    else:
        bp_ref, bs_ref, w_ref, o_ref = rest
    i = pl.program_id(0)

    def body(b_ref):
        if with_pool:
            mix = _dot(a_ref[...], w_ref[:A_WIDTH, :]) + _dot(b_ref[...], w_ref[A_WIDTH:, :])
        else:
            mix = _dot(b_ref[...], w_ref[...])
        o_ref[...] = x_ref[...] + mod_ref[5:6, :] * mix

    pl.when(i < npt)(lambda: body(bp_ref))
    pl.when(i >= npt)(lambda: body(bs_ref))


def _mix_out(tok, x, mod, a, b_prompt, b_sample, w, layer):
    with_pool = a is not None
    bw = b_prompt.shape[1]
    idx = layer // 2
    in_specs = [tok.x_spec(), tok.mod_spec(layer)]
    args = [x, mod]
    if with_pool:
        in_specs.append(tok.x_spec(A_WIDTH))
        args.append(a)
    in_specs += [tok.prompt_spec(bw), tok.sample_spec(bw), _resident((None,) + w.shape[1:], lambda i: (idx, 0, 0))]
    args += [b_prompt, b_sample, w]
    return pl.pallas_call(
        functools.partial(_out_kernel, npt=tok.npt, with_pool=with_pool),
        out_shape=jax.ShapeDtypeStruct((tok.t, D_MODEL), F32),
        grid=(tok.nt,),
        in_specs=in_specs,
        out_specs=tok.x_spec(),
        compiler_params=_cparams("arbitrary"),
        name="mix_out_ab" if with_pool else "mix_out_c",
    )(*args)


def _ctx_heads(cache, with_ones=False):
    t = jnp.transpose(cache, (0, 2, 1, 3))
    if with_ones:
        t = jnp.concatenate([t, jnp.ones_like(t)], axis=-1)
    return t.astype(BF16)


def kernel(x_prompt, x_sample, cache_nb_k, cache_nb_v, cache_attn_k, cache_attn_v, c, c_ctx, w_mod, b_mod, g_norm, w_ffn_in, w_ffn_out, w_in_ab, w_pool, pool_scale, nb_rpb, w_out_ab, w_qkv_c, g_qnorm, g_knorm, w_out_c, g_final):
    bp, lp, d = x_prompt.shape
    bs, ls, _ = x_sample.shape
    assert d == D_MODEL and w_ffn_in.shape[-1] == 2 * D_FF and w_mod.shape[0] == DEPTH
    tok = _Tokens(bp, lp, bs, ls)

    n_rows = SUBLANES * (-(-(1 + bs) // SUBLANES))
    cvec = jnp.concatenate([c_ctx[None], c, jnp.zeros((n_rows - 1 - bs, d), F32)], axis=0)
    mod = _ada(cvec, w_mod, b_mod).reshape(DEPTH, n_rows, N_MOD, d)
    g_norm4 = g_norm.reshape(DEPTH, 3, 1, d)

    w_ffn_in_b = w_ffn_in.astype(BF16)
    w_ffn_out_b = w_ffn_out.astype(BF16)
    w_in_ab_b = w_in_ab.astype(BF16)
    w_pool_b = w_pool.astype(BF16)
    w_out_ab_b = w_out_ab.astype(BF16)
    w_qkv_c_b = w_qkv_c.astype(BF16)
    w_out_c_b = w_out_c.astype(BF16)
    cos_t, sin_t = _rope_tables(tok)

    x = (x_prompt.reshape(tok.tp, d), x_sample.reshape(tok.ts, d))
    nb_k, nb_v, at_k, at_v = [], [], [], []
    for l in range(DEPTH):
        x = _ffn(tok, x, mod, g_norm4, w_ffn_in_b, w_ffn_out_b, l, 0)
        if l % 2 == 0:
            e = l // 2
            u, q, k, v, kf, vf = _proj_ab(tok, x, mod, g_norm4, w_in_ab_b, l)
            a = _pool(tok, u, w_pool_b, pool_scale, e)
            b_p = _prompt_attn(tok, q, k, v, NB_HEADS, NB_HEADS)
            rel, rowmask = _nb_bias_tables(nb_rpb[e], ls // GRID_W)
            b_s = _nb_attn(tok, q, k, v, _ctx_heads(cache_nb_k[:, e]), _ctx_heads(cache_nb_v[:, e], True),
                           rel, rowmask)
            x = _mix_out(tok, x, mod, a, b_p, b_s, w_out_ab_b, l)
            nb_k.append(kf.reshape(bp, lp, NB_HEADS, HEAD_DIM))
            nb_v.append(vf.reshape(bp, lp, NB_HEADS, HEAD_DIM))
        else:
            o = l // 2
            q, k, v, kf, vf = _proj_c(tok, x, mod, g_norm4, w_qkv_c_b, g_qnorm, g_knorm, cos_t, sin_t, l)
            b_p = _prompt_attn(tok, q, k, v, C_Q_HEADS, C_KV_HEADS)
            b_s = _gqa_attn(tok, q, k, v, _ctx_heads(cache_attn_k[:, o]), _ctx_heads(cache_attn_v[:, o], True))
            x = _mix_out(tok, x, mod, None, b_p, b_s, w_out_c_b, l)
            at_k.append(kf.reshape(bp, lp, C_KV_HEADS, HEAD_DIM))
            at_v.append(vf.reshape(bp, lp, C_KV_HEADS, HEAD_DIM))
        x = _ffn(tok, x, mod, g_norm4, w_ffn_in_b, w_ffn_out_b, l, 2, g_final=g_final if l == DEPTH - 1 else None)

    y_prompt, y_sample = x
    return (y_prompt.reshape(bp, lp, d), y_sample.reshape(bs, ls, d), jnp.stack(nb_k, axis=1),
            jnp.stack(nb_v, axis=1), jnp.stack(at_k, axis=1), jnp.stack(at_v, axis=1))
```

```python
import functools
import math

import numpy as np
import jax
import jax.numpy as jnp
from jax import lax
from jax.experimental import pallas as pl
from jax.experimental.pallas import tpu as pltpu

F32 = jnp.float32
BF16 = jnp.bfloat16

D_MODEL = 1024
DEPTH = 4
GRID_W = 64
D_FF = 2816
N_MOD = 9
A_WIDTH = 512
POOL_WINDOWS = (2, 4, 8, 16)
POOL_GROUP = 128
HEAD_DIM = 64
NB_HEADS = 8
NB_ROWS = 8
NB_COLS = 16
C_Q_HEADS = 16
C_KV_HEADS = 4
ROPE_THETA = 10000.0
EPS = 1e-6
NEG_INF = -1e30
LOG2E = math.log2(math.e)
Q_SCALE = HEAD_DIM ** -0.5 * LOG2E

LANES = 128
SUBLANES = 8
VMEM_LIMIT_BYTES = 56 * 1024 * 1024

TM = 512
TF = 256
TPOOL = 256
POOL_HALO = 8
NB_BAND = 8
NB_KROWS = 16
GQ_TOK = 128
PIPE_UNROLL = 4
KCHUNK = 256


def _cparams(*sem):
    return pltpu.CompilerParams(dimension_semantics=sem, vmem_limit_bytes=VMEM_LIMIT_BYTES)


def _resident(shape, index_map):
    return pl.BlockSpec(shape, index_map, pipeline_mode=pl.Buffered(1))


def _dot(a, b):
    return jnp.dot(a, b, preferred_element_type=F32)


def _dot_nt(a, b):
    return lax.dot_general(a, b, (((1,), (1,)), ((), ())), preferred_element_type=F32)


def _norm_mod(x, g, shift, scale):
    ms = jnp.mean(x * x, axis=-1, keepdims=True)
    y = x * lax.rsqrt(ms + EPS) * g
    return y * (1.0 + scale) + shift


def _with_ones(v):
    return jnp.concatenate([v, jnp.ones_like(v)], axis=-1).astype(BF16)


def _ada_kernel(c_ref, w_ref, b_ref, o_ref):
    c = c_ref[...]
    s = (c * jax.nn.sigmoid(c)).astype(BF16)
    o_ref[...] = _dot(s, w_ref[...].astype(BF16)) + b_ref[...]


def _ada(cvec, w_mod, b_mod):
    rows = cvec.shape[0]
    n = w_mod.shape[-1]
    tn = D_MODEL
    return pl.pallas_call(
        _ada_kernel,
        out_shape=jax.ShapeDtypeStruct((DEPTH, rows, n), F32),
        grid=(DEPTH, n // tn),
        in_specs=[
            pl.BlockSpec((rows, D_MODEL), lambda l, j: (0, 0)),
            pl.BlockSpec((None, D_MODEL, tn), lambda l, j: (l, 0, j)),
            pl.BlockSpec((None, 1, tn), lambda l, j: (l, 0, j)),
        ],
        out_specs=pl.BlockSpec((None, rows, tn), lambda l, j: (l, 0, j)),
        compiler_params=_cparams("arbitrary", "arbitrary"),
        name="ada_mod",
    )(cvec, w_mod, b_mod.reshape(DEPTH, 1, n))


class _Tokens:
    def __init__(self, bp, lp, bs, ls):
        self.bp, self.lp, self.bs, self.ls = bp, lp, bs, ls
        self.tp = bp * lp
        self.ts = bs * ls
        self.t = self.tp + self.ts
        assert self.tp % TM == 0 and ls % TM == 0 and lp % TPOOL == 0 and ls % TPOOL == 0
        self.npt = self.tp // TM
        self.nt = self.t // TM
        self.tiles_per_seq = ls // TM

    def mod_row(self, i):
        return jnp.where(i < self.npt, 0, 1 + (i - self.npt) // self.tiles_per_seq)

    def x_spec(self, width=D_MODEL):
        return pl.BlockSpec((TM, width), lambda i: (i, 0))

    def mod_spec(self, layer):
        return pl.BlockSpec((None, None, N_MOD, D_MODEL), lambda i: (layer, self.mod_row(i), 0, 0))

    def prompt_spec(self, width):
        return pl.BlockSpec((TM, width), lambda i: (jnp.minimum(i, self.npt - 1), 0))

    def sample_spec(self, width):
        return pl.BlockSpec((TM, width), lambda i: (jnp.maximum(i - self.npt, 0), 0))


def _vec_spec(layer, sub):
    return pl.BlockSpec((None, None, 1, D_MODEL), lambda i: (layer, sub, 0, 0))


def _ffn_kernel(*refs, sub, npt, split_in, mix, final):
    refs = list(refs)
    i = pl.program_id(0)
    if split_in:
        xp_ref, xs_ref = refs[:2]
        refs = refs[2:]
        x = jnp.where(i < npt, xp_ref[...], xs_ref[...])
    else:
        x = refs.pop(0)[...]
    if mix is not None:
        a_ref = refs.pop(0) if mix == "pool+attn" else None
        bp_ref, bs_ref, wmix_ref = refs[:3]
        refs = refs[3:]
    mod_ref, g_ref, win_ref, wout_ref = refs[:4]
    refs = refs[4:]
    if final:
        gf_ref, yp_ref, ys_ref = refs
    else:
        (o_ref,) = refs
    if mix is not None:
        b = jnp.where(i < npt, bp_ref[...], bs_ref[...])
        if a_ref is None:
            mixed = _dot(b, wmix_ref[...])
        else:
            mixed = _dot(a_ref[...], wmix_ref[:A_WIDTH, :]) + _dot(b, wmix_ref[A_WIDTH:, :])
        x = x + mod_ref[5:6, :] * mixed
    h = _norm_mod(x, g_ref[...], mod_ref[3 * sub:3 * sub + 1, :], mod_ref[3 * sub + 1:3 * sub + 2, :]).astype(BF16)
    acc = None
    for j in range(D_FF // TF):
        a = _dot(h, win_ref[:, j * TF:(j + 1) * TF].astype(BF16))
        b = _dot(h, win_ref[:, D_FF + j * TF:D_FF + (j + 1) * TF].astype(BF16))
        act = (a * jax.nn.sigmoid(a) * b).astype(BF16)
        t = _dot(act, wout_ref[j * TF:(j + 1) * TF, :].astype(BF16))
        acc = t if acc is None else acc + t
    y = x + mod_ref[3 * sub + 2:3 * sub + 3, :] * (0.5 * acc)
    if final:
        ms = jnp.mean(y * y, axis=-1, keepdims=True)
        y = y * lax.rsqrt(ms + EPS) * gf_ref[...]

        @pl.when(i < npt)
        def _():
            yp_ref[...] = y

        @pl.when(i >= npt)
        def _():
            ys_ref[...] = y
    else:
        o_ref[...] = y


def _ffn(tok, x, mod, g_norm, w_in, w_out, layer, sub, g_final=None, mixer=None):
    split_in = isinstance(x, tuple)
    final = g_final is not None
    which = sub // 2
    mix = None
    if split_in:
        in_specs = [tok.prompt_spec(D_MODEL), tok.sample_spec(D_MODEL)]
        args = list(x)
    else:
        in_specs = [tok.x_spec()]
        args = [x]
    if mixer is not None:
        a, b_prompt, b_sample, w_mix = mixer
        mix = "attn" if a is None else "pool+attn"
        bw = b_prompt.shape[1]
        if a is not None:
            in_specs.append(tok.x_spec(A_WIDTH))
            args.append(a)
        in_specs += [tok.prompt_spec(bw), tok.sample_spec(bw),
                     _resident((None,) + w_mix.shape[1:], lambda i: (layer // 2, 0, 0))]
        args += [b_prompt, b_sample, w_mix]
    in_specs += [
        tok.mod_spec(layer), _vec_spec(layer, sub),
        _resident((None, None, D_MODEL, 2 * D_FF), lambda i: (layer, which, 0, 0)),
        _resident((None, None, D_FF, D_MODEL), lambda i: (layer, which, 0, 0)),
    ]
    args += [mod, g_norm, w_in, w_out]
    if final:
        in_specs.append(pl.BlockSpec((1, D_MODEL), lambda i: (0, 0)))
        args.append(g_final.reshape(1, D_MODEL))
        out_shape = (jax.ShapeDtypeStruct((tok.tp, D_MODEL), F32), jax.ShapeDtypeStruct((tok.ts, D_MODEL), F32))
        out_specs = (tok.prompt_spec(D_MODEL), tok.sample_spec(D_MODEL))
    else:
        out_shape = jax.ShapeDtypeStruct((tok.t, D_MODEL), F32)
        out_specs = tok.x_spec()
    return pl.pallas_call(
        functools.partial(_ffn_kernel, sub=sub, npt=tok.npt, split_in=split_in, mix=mix, final=final),
        out_shape=out_shape,
        grid=(tok.nt,),
        in_specs=in_specs,
        out_specs=out_specs,
        compiler_params=_cparams("arbitrary"),
        name="ffn_final" if final else ("ffn_first" if split_in else ("ffn" if mix is None else "mix_ffn")),
    )(*args)


def _proj_ab_kernel(x_ref, mod_ref, g_ref, w_ref, u_ref, q_ref, k_ref, v_ref, kf_ref, vf_ref, *, npt):
    i = pl.program_id(0)
    h = _norm_mod(x_ref[...], g_ref[...], mod_ref[3:4, :], mod_ref[4:5, :]).astype(BF16)
    proj = _dot(h, w_ref[...])
    u_ref[...] = proj[:, :A_WIDTH]
    hw = NB_HEADS * HEAD_DIM
    for hh in range(NB_HEADS):
        lo = A_WIDTH + hh * HEAD_DIM
        q_ref[hh] = (proj[:, lo:lo + HEAD_DIM] * Q_SCALE).astype(BF16)
        k_ref[hh] = proj[:, lo + hw:lo + hw + HEAD_DIM].astype(BF16)
        v_ref[hh] = _with_ones(proj[:, lo + 2 * hw:lo + 2 * hw + HEAD_DIM])

    @pl.when(i < npt)
    def _():
        kf_ref[...] = proj[:, A_WIDTH + hw:A_WIDTH + 2 * hw]
        vf_ref[...] = proj[:, A_WIDTH + 2 * hw:A_WIDTH + 3 * hw]


def _head_major(n_heads, tok, width=HEAD_DIM):
    return (pl.BlockSpec((n_heads, TM, width), lambda i: (0, i, 0)),
            jax.ShapeDtypeStruct((n_heads, tok.t, width), BF16))


def _proj_ab(tok, x, mod, g_norm, w, layer):
    e = layer // 2
    hw = NB_HEADS * HEAD_DIM
    (qs, qsh), (ks, ksh), (vs, vsh) = (_head_major(NB_HEADS, tok), _head_major(NB_HEADS, tok),
                                       _head_major(NB_HEADS, tok, 2 * HEAD_DIM))
    return pl.pallas_call(
        functools.partial(_proj_ab_kernel, npt=tok.npt),
        out_shape=(jax.ShapeDtypeStruct((tok.t, A_WIDTH), F32), qsh, ksh, vsh,
                   jax.ShapeDtypeStruct((tok.tp, hw), F32), jax.ShapeDtypeStruct((tok.tp, hw), F32)),
        grid=(tok.nt,),
        in_specs=[tok.x_spec(), tok.mod_spec(layer), _vec_spec(layer, 1),
                  _resident((None,) + w.shape[1:], lambda i: (e, 0, 0))],
        out_specs=(tok.x_spec(A_WIDTH), qs, ks, vs, tok.prompt_spec(hw), tok.prompt_spec(hw)),
        compiler_params=_cparams("arbitrary"),
        name="proj_ab",
    )(x, mod, g_norm, w)


def _pool_kernel(u_ref, prev_ref, next_ref, w_ref, sc_ref, o_ref, *, n_prompt_tiles, lp, ls):
    i = pl.program_id(0)
    is_p = i < n_prompt_tiles
    tiles_s = ls // TPOOL
    j = jnp.where(is_p, 0, (i - n_prompt_tiles) % tiles_s)
    seq_len = jnp.where(is_p, lp, ls)
    pos0 = j * TPOOL
    first = j == 0
    last = pos0 + TPOOL == seq_len
    prev = jnp.where(first, 0.0, prev_ref[...])
    nxt = jnp.where(last, 0.0, next_ref[...])
    ext = jnp.concatenate([prev, u_ref[...], nxt], axis=0)
    n_ext = TPOOL + 2 * POOL_HALO
    pos = pos0 + lax.broadcasted_iota(jnp.int32, (TPOOL, 1), 0)

    def fwd(v, d):
        return pltpu.roll(v, n_ext - d, 0)

    for gi, w in enumerate(POOL_WINDOWS):
        e = ext[:, gi * POOL_GROUP:(gi + 1) * POOL_GROUP]
        acc = e
        span = 1
        while span < w:
            acc = acc + fwd(acc, span)
            span *= 2
        half = w // 2
        centred = pltpu.roll(acc, half, 0)
        s = centred[POOL_HALO:POOL_HALO + TPOOL]
        cnt = (jnp.minimum(pos + (w - 1 - half), seq_len - 1) - jnp.maximum(pos - half, 0) + 1).astype(F32)
        pooled = (s / cnt - e[POOL_HALO:POOL_HALO + TPOOL]).astype(BF16)
        mixed = _dot(pooled, w_ref[gi])
        o_ref[:, gi * POOL_GROUP:(gi + 1) * POOL_GROUP] = (
            mixed * sc_ref[:, gi * POOL_GROUP:(gi + 1) * POOL_GROUP]).astype(BF16)


def _pool(tok, u, w_pool, pool_scale, e):
    n_tiles = tok.t // TPOOL
    per = TPOOL // POOL_HALO
    n_halo_blocks = tok.t // POOL_HALO
    return pl.pallas_call(
        functools.partial(_pool_kernel, n_prompt_tiles=tok.tp // TPOOL, lp=tok.lp, ls=tok.ls),
        out_shape=jax.ShapeDtypeStruct((tok.t, A_WIDTH), BF16),
        grid=(n_tiles,),
        in_specs=[
            pl.BlockSpec((TPOOL, A_WIDTH), lambda i: (i, 0)),
            pl.BlockSpec((POOL_HALO, A_WIDTH), lambda i: (jnp.maximum(i * per - 1, 0), 0)),
            pl.BlockSpec((POOL_HALO, A_WIDTH), lambda i: (jnp.minimum((i + 1) * per, n_halo_blocks - 1), 0)),
            pl.BlockSpec((None, len(POOL_WINDOWS), POOL_GROUP, POOL_GROUP), lambda i: (e, 0, 0, 0)),
            pl.BlockSpec((None, 1, A_WIDTH), lambda i: (e, 0, 0)),
        ],
        out_specs=pl.BlockSpec((TPOOL, A_WIDTH), lambda i: (i, 0)),
        compiler_params=_cparams("arbitrary"),
        name="pool_mix",
    )(u, u, u, w_pool, pool_scale.reshape(pool_scale.shape[0], 1, A_WIDTH))


def _normalised(pv, upper):
    swapped = pltpu.roll(pv, HEAD_DIM, 1)
    return swapped / pv if upper else pv / swapped


def _pair_lanes(lower_src, upper_src):
    lane = lax.broadcasted_iota(jnp.int32, lower_src.shape, 1)
    return jnp.where(lane < HEAD_DIM, lower_src, upper_src)


def _attention_pipeline(n_blocks, unroll, q_of, chunks_of, store, s_refs, mx_ref, pv_ref):
    assert n_blocks % unroll == 0 and unroll % 2 == 0

    def scores(n, pos, slot):
        q = q_of(n, pos)
        mx = None
        for c, (k_fn, _, bias_fn) in enumerate(chunks_of(n, pos)):
            s = _dot_nt(q, k_fn())
            if bias_fn is not None:
                s = s + bias_fn()
            s_refs[slot][:, c * KCHUNK:(c + 1) * KCHUNK] = s
            part = jnp.maximum(s[:, :LANES], s[:, LANES:])
            mx = part if mx is None else jnp.maximum(mx, part)
        mx_ref[slot] = mx

    def outputs(n, pos, slot):
        mx = mx_ref[slot]
        m = jnp.broadcast_to(mx.max(axis=-1, keepdims=True), mx.shape)
        pv = None
        for c, (_, v_fn, _) in enumerate(chunks_of(n, pos)):
            p = jnp.concatenate(
                [jnp.exp2(s_refs[slot][:, c * KCHUNK + h * LANES:c * KCHUNK + (h + 1) * LANES] - m)
                 for h in range(KCHUNK // LANES)], axis=-1).astype(BF16)
            t = _dot(p, v_fn())
            pv = t if pv is None else pv + t
        pv_ref[slot] = pv

    def finalize(n, pos, slot):
        store(n, pos, pv_ref[slot])

    if n_blocks == unroll:
        scores(0, 0, 0)
        for u in range(unroll):
            if u >= 2:
                finalize(u - 2, u - 2, u % 2)
            outputs(u, u, u % 2)
            if u + 1 < unroll:
                scores(u + 1, u + 1, 1 - u % 2)
        finalize(unroll - 2, unroll - 2, 0)
        finalize(unroll - 1, unroll - 1, 1)
        return

    pv_ref[...] = jnp.ones(pv_ref.shape, F32)
    scores(0, 0, 0)

    def body(j, carry):
        base = unroll * j
        for u in range(unroll):
            slot = u % 2
            pf = (u - 2) % unroll
            finalize(jnp.maximum(base + u - 2, pf), pf, slot)
            outputs(base + u, u, slot)
            ps = (u + 1) % unroll
            scores(jnp.minimum(base + u + 1, n_blocks - unroll + ps), ps, 1 - slot)
        return carry

    lax.fori_loop(0, n_blocks // unroll, body, 0)
    finalize(n_blocks - 2, unroll - 2, 0)
    finalize(n_blocks - 1, unroll - 1, 1)


def _pipeline_scratch(m_rows, n_keys):
    return [pltpu.VMEM((m_rows, n_keys), F32), pltpu.VMEM((m_rows, n_keys), F32),
            pltpu.VMEM((2, m_rows, LANES), F32), pltpu.VMEM((2, m_rows, LANES), F32)]


def _prompt_attn_kernel(q_ref, k_ref, v_ref, o_ref, s0_ref, s1_ref, mx_ref, pv_ref, *, group):
    n_kv = k_ref.shape[0]
    seq = q_ref.shape[1]

    def q_of(n, pos):
        return jnp.concatenate([q_ref[group * pos + g] for g in range(group)], axis=0) if group > 1 else q_ref[pos]

    def chunks_of(n, pos):
        return [(functools.partial(lambda c0: k_ref[pos, c0:c0 + KCHUNK, :], c0),
                 functools.partial(lambda c0: v_ref[pos, c0:c0 + KCHUNK, :], c0), None)
                for c0 in range(0, seq, KCHUNK)]

    def store(n, pos, pv):
        lower, upper = _normalised(pv, False), _normalised(pv, True)
        for g in range(group):
            head = group * pos + g
            src = upper if head % 2 else lower
            half = (head % 2) * HEAD_DIM
            o_ref[:, head * HEAD_DIM:(head + 1) * HEAD_DIM] = (
                src[g * seq:(g + 1) * seq, half:half + HEAD_DIM].astype(BF16))

    _attention_pipeline(n_kv, n_kv, q_of, chunks_of, store, (s0_ref, s1_ref), mx_ref, pv_ref)


def _prompt_attn(tok, q, k, v, n_q_heads, n_kv_heads):
    group = n_q_heads // n_kv_heads
    assert tok.lp % KCHUNK == 0 and n_kv_heads % 2 == 0
    return pl.pallas_call(
        functools.partial(_prompt_attn_kernel, group=group),
        out_shape=jax.ShapeDtypeStruct((tok.tp, n_q_heads * HEAD_DIM), BF16),
        grid=(tok.bp,),
        in_specs=[
            pl.BlockSpec((n_q_heads, tok.lp, HEAD_DIM), lambda b: (0, b, 0)),
            pl.BlockSpec((n_kv_heads, tok.lp, HEAD_DIM), lambda b: (0, b, 0)),
            pl.BlockSpec((n_kv_heads, tok.lp, 2 * HEAD_DIM), lambda b: (0, b, 0)),
        ],
        out_specs=pl.BlockSpec((tok.lp, n_q_heads * HEAD_DIM), lambda b: (b, 0)),
        scratch_shapes=_pipeline_scratch(group * tok.lp, tok.lp),
        compiler_params=_cparams("arbitrary"),
        name="prompt_attn",
    )(q, k, v)


def _nb_attn_kernel(q_ref, k_ref, v_ref, ck_ref, cv_ref, rel_ref, rowmask_ref, o_ref,
                    s0_ref, s1_ref, mx_ref, pv_ref, *, rows, n_ctx):
    n_bands = rows // NB_BAND
    tq = NB_BAND * GRID_W

    def place(n, pos):
        return pos // 2, 2 * (n // PIPE_UNROLL) + pos % 2

    def q_of(n, pos):
        hh, band = place(n, pos)
        return q_ref[hh, pl.ds(pl.multiple_of(band * tq, tq), tq), :]

    def chunks_of(n, pos):
        hh, band = place(n, pos)
        ty = jnp.where(band == 0, 0, jnp.where(band == n_bands - 1, 2, 1))
        row0 = jnp.clip(NB_BAND * band - NB_ROWS // 2, 0, rows - NB_KROWS)
        start = row0 * GRID_W

        def bias(c0):
            strips = []
            for j in range(NB_BAND):
                off = (NB_BAND - 1 - j) * GRID_W
                copy = (off // GRID_W) % 2
                lo = off - copy * GRID_W + c0
                strips.append(rel_ref[ty, hh, copy, :, lo:lo + KCHUNK])
            return jnp.concatenate(strips, axis=0) + rowmask_ref[ty, :, c0:c0 + KCHUNK]

        def band_rows(ref, c0):
            return ref[hh, pl.ds(pl.multiple_of(start + c0, KCHUNK), KCHUNK), :]

        out = [(functools.partial(band_rows, k_ref, c0), functools.partial(band_rows, v_ref, c0),
                functools.partial(bias, c0)) for c0 in range(0, NB_KROWS * GRID_W, KCHUNK)]
        out += [(functools.partial(lambda c0: ck_ref[hh, c0:c0 + KCHUNK, :], c0),
                 functools.partial(lambda c0: cv_ref[hh, c0:c0 + KCHUNK, :], c0), None)
                for c0 in range(0, n_ctx, KCHUNK)]
        return out

    def store(n, pos, pv):
        hh, band = place(n, pos)
        half = hh * HEAD_DIM
        o_ref[pl.ds(pl.multiple_of(band * tq, tq), tq), half:half + HEAD_DIM] = (
            _normalised(pv, hh == 1)[:, half:half + HEAD_DIM].astype(BF16))

    _attention_pipeline(2 * n_bands, PIPE_UNROLL, q_of, chunks_of, store, (s0_ref, s1_ref), mx_ref, pv_ref)


def _nb_attn(tok, q, k, v, ck, cv, rel, rowmask):
    rows = tok.ls // GRID_W
    tq = NB_BAND * GRID_W
    kv0 = tok.tp // tok.ls
    past = ck.shape[2]
    assert tok.tp % tok.ls == 0 and past % KCHUNK == 0 and (rows // NB_BAND) % 2 == 0
    return pl.pallas_call(
        functools.partial(_nb_attn_kernel, rows=rows, n_ctx=past),
        out_shape=jax.ShapeDtypeStruct((tok.ts, NB_HEADS * HEAD_DIM), BF16),
        grid=(tok.bs, NB_HEADS // 2),
        in_specs=[
            pl.BlockSpec((2, tok.ls, HEAD_DIM), lambda b, p: (p, kv0 + b, 0)),
            pl.BlockSpec((2, tok.ls, HEAD_DIM), lambda b, p: (p, kv0 + b, 0)),
            pl.BlockSpec((2, tok.ls, 2 * HEAD_DIM), lambda b, p: (p, kv0 + b, 0)),
            pl.BlockSpec((None, 2, past, HEAD_DIM), lambda b, p: (b, p, 0, 0)),
            pl.BlockSpec((None, 2, past, 2 * HEAD_DIM), lambda b, p: (b, p, 0, 0)),
            pl.BlockSpec((3, 2, 2, GRID_W, rel.shape[-1]), lambda b, p: (0, p, 0, 0, 0)),
            _resident(rowmask.shape, lambda b, p: (0, 0, 0)),
        ],
        out_specs=pl.BlockSpec((tok.ls, 2 * HEAD_DIM), lambda b, p: (b, p)),
        scratch_shapes=_pipeline_scratch(tq, NB_KROWS * GRID_W + past),
        compiler_params=_cparams("arbitrary", "arbitrary"),
        name="nb_attn",
    )(q, k, v, ck, cv, rel, rowmask)


def _nb_band_geometry(rows):
    n_bands = rows // NB_BAND
    offs = []
    valid = np.zeros((3, NB_BAND, NB_KROWS), bool)
    for ty, band in enumerate((0, 1, n_bands - 1)):
        row0 = int(np.clip(NB_BAND * band - NB_ROWS // 2, 0, rows - NB_KROWS))
        offs.append(row0 - NB_BAND * band + NB_ROWS - 1)
        for jq in range(NB_BAND):
            r = NB_BAND * band + jq
            rs = int(np.clip(r - NB_ROWS // 2, 0, rows - NB_ROWS))
            for m in range(NB_KROWS):
                valid[ty, jq, m] = rs <= row0 + m < rs + NB_ROWS
    return offs, valid


def _toeplitz_cols(rpb):
    w = GRID_W
    n_edge = w - NB_COLS
    lead = rpb.shape[:-1]
    e = jnp.concatenate([jnp.broadcast_to(rpb[..., :1], lead + (n_edge,)), rpb,
                         jnp.broadcast_to(rpb[..., -1:], lead + (n_edge + 1,))], axis=-1)
    flat = jnp.tile(e, (1,) * len(lead) + (w,))[..., :w * (2 * w - 1)]
    return flat.reshape(lead + (w, 2 * w - 1))[..., w - 1:]


def _nb_bias_tables(rpb, rows):
    n_heads = rpb.shape[0]
    cq = np.arange(GRID_W)
    cstart = np.clip(cq - NB_COLS // 2, 0, GRID_W - NB_COLS)
    col_valid = (cq[None, :] >= cstart[:, None]) & (cq[None, :] < cstart[:, None] + NB_COLS)
    blocks = jnp.where(col_valid[None, None], _toeplitz_cols(rpb) * LOG2E, NEG_INF)
    n_d = 2 * NB_ROWS - 1
    blocks = jnp.concatenate([blocks, jnp.zeros((n_heads, 1, GRID_W, GRID_W), F32)], axis=1)
    offs, valid = _nb_band_geometry(rows)
    n_strip = NB_BAND + NB_KROWS
    rel = []
    for c in offs:
        copies = []
        for copy in range(2):
            d = np.arange(n_strip) + c - (NB_BAND - 1) + copy
            d = np.where((d >= 0) & (d < n_d), d, n_d)
            t = jnp.take(blocks, jnp.asarray(d), axis=1)
            copies.append(jnp.transpose(t, (0, 2, 1, 3)).reshape(n_heads, GRID_W, n_strip * GRID_W))
        rel.append(jnp.stack(copies, axis=1))
    rel = jnp.stack(rel, axis=0)
    small = jnp.where(jnp.asarray(valid), 0.0, NEG_INF).astype(F32)
    rowmask = jnp.broadcast_to(small[:, :, None, :, None], (3, NB_BAND, GRID_W, NB_KROWS, GRID_W))
    return rel, rowmask.reshape(3, NB_BAND * GRID_W, NB_KROWS * GRID_W)


def _gqa_attn_kernel(q_ref, k_ref, v_ref, ck_ref, cv_ref, o_ref, s0_ref, s1_ref, mx_ref, pv_ref,
                     *, group, n_ctx, n_self):
    def q_of(n, pos):
        r0 = pl.multiple_of(n * GQ_TOK, GQ_TOK)
        return jnp.concatenate([q_ref[g, pl.ds(r0, GQ_TOK), :] for g in range(group)], axis=0)

    def chunks_of(n, pos):
        out = [(functools.partial(lambda c0: ck_ref[0, c0:c0 + KCHUNK, :], c0),
                functools.partial(lambda c0: cv_ref[0, c0:c0 + KCHUNK, :], c0), None)
               for c0 in range(0, n_ctx, KCHUNK)]
        out += [(functools.partial(lambda c0: k_ref[0, c0:c0 + KCHUNK, :], c0),
                 functools.partial(lambda c0: v_ref[0, c0:c0 + KCHUNK, :], c0), None)
                for c0 in range(0, n_self, KCHUNK)]
        return out

    def store(n, pos, pv):
        r0 = pl.multiple_of(n * GQ_TOK, GQ_TOK)
        lower, upper = _normalised(pv, False), _normalised(pv, True)
        pairs = [_pair_lanes(lower[g * GQ_TOK:(g + 1) * GQ_TOK], upper[(g + 1) * GQ_TOK:(g + 2) * GQ_TOK])
                 for g in range(0, group, 2)]
        o_ref[pl.ds(r0, GQ_TOK), :] = jnp.concatenate(pairs, axis=-1).astype(BF16)

    _attention_pipeline(q_ref.shape[1] // GQ_TOK, PIPE_UNROLL, q_of, chunks_of, store,
                        (s0_ref, s1_ref), mx_ref, pv_ref)


def _gqa_attn(tok, q, k, v, ck, cv):
    group = C_Q_HEADS // C_KV_HEADS
    kv0 = tok.tp // tok.ls
    past = ck.shape[2]
    assert past % KCHUNK == 0 and tok.ls % KCHUNK == 0 and tok.ls % (GQ_TOK * PIPE_UNROLL) == 0 and group % 2 == 0
    return pl.pallas_call(
        functools.partial(_gqa_attn_kernel, group=group, n_ctx=past, n_self=tok.ls),
        out_shape=jax.ShapeDtypeStruct((tok.ts, C_Q_HEADS * HEAD_DIM), BF16),
        grid=(tok.bs, C_KV_HEADS),
        in_specs=[
            pl.BlockSpec((group, tok.ls, HEAD_DIM), lambda b, g: (g, kv0 + b, 0)),
            pl.BlockSpec((1, tok.ls, HEAD_DIM), lambda b, g: (g, kv0 + b, 0)),
            pl.BlockSpec((1, tok.ls, 2 * HEAD_DIM), lambda b, g: (g, kv0 + b, 0)),
            pl.BlockSpec((None, 1, past, HEAD_DIM), lambda b, g: (b, g, 0, 0)),
            pl.BlockSpec((None, 1, past, 2 * HEAD_DIM), lambda b, g: (b, g, 0, 0)),
        ],
        out_specs=pl.BlockSpec((tok.ls, group * HEAD_DIM), lambda b, g: (b, g)),
        scratch_shapes=_pipeline_scratch(group * GQ_TOK, past + tok.ls),
        compiler_params=_cparams("arbitrary", "arbitrary"),
        name="gqa_attn",
    )(q, k, v, ck, cv)


def _partner(x, s, lane):
    return jnp.where((lane & s) != 0, pltpu.roll(x, s, 1), pltpu.roll(x, LANES - s, 1))


def _head_rms(x, g, seg_ones):
    y = x * x
    hi = y.astype(BF16)
    lo = (y - hi.astype(F32)).astype(BF16)
    ss = _dot(hi, seg_ones) + _dot(lo, seg_ones)
    return x * lax.rsqrt(ss * (1.0 / HEAD_DIM) + EPS) * g


def _rope(x, cos, sin, lane):
    return x * cos + _partner(x, HEAD_DIM // 4, lane) * sin


def _proj_c_kernel(x_ref, mod_ref, g_ref, w_ref, gq_ref, gk_ref, cos_ref, sin_ref,
                   q_ref, k_ref, v_ref, kf_ref, vf_ref, *, npt):
    i = pl.program_id(0)
    h = _norm_mod(x_ref[...], g_ref[...], mod_ref[3:4, :], mod_ref[4:5, :]).astype(BF16)
    proj = _dot(h, w_ref[...])
    nq = C_Q_HEADS * HEAD_DIM
    nk = C_KV_HEADS * HEAD_DIM
    lane = lax.broadcasted_iota(jnp.int32, (TM, LANES), 1)
    seg_ones = (lax.broadcasted_iota(jnp.int32, (LANES, LANES), 0) // HEAD_DIM ==
                lax.broadcasted_iota(jnp.int32, (LANES, LANES), 1) // HEAD_DIM).astype(BF16)
    cos = cos_ref[...]
    sin = sin_ref[...]
    gq = gq_ref[...]
    gk = gk_ref[...]
    for c in range(nq // LANES):
        y = _head_rms(proj[:, c * LANES:(c + 1) * LANES], gq, seg_ones)
        y = _rope(y, cos, sin, lane) * Q_SCALE
        q_ref[2 * c] = y[:, :HEAD_DIM].astype(BF16)
        q_ref[2 * c + 1] = y[:, HEAD_DIM:].astype(BF16)
    for c in range(nk // LANES):
        yk = _head_rms(proj[:, nq + c * LANES:nq + (c + 1) * LANES], gk, seg_ones)

        @pl.when(i < npt)
        def _(yk=yk, c=c):
            kf_ref[:, c * LANES:(c + 1) * LANES] = yk

        yr = _rope(yk, cos, sin, lane)
        k_ref[2 * c] = yr[:, :HEAD_DIM].astype(BF16)
        k_ref[2 * c + 1] = yr[:, HEAD_DIM:].astype(BF16)
    for hh in range(C_KV_HEADS):
        v_ref[hh] = _with_ones(proj[:, nq + nk + hh * HEAD_DIM:nq + nk + (hh + 1) * HEAD_DIM])

    @pl.when(i < npt)
    def _():
        vf_ref[...] = proj[:, nq + nk:nq + 2 * nk]


def _rope_tables(tok):
    n = HEAD_DIM // 4
    inv = ROPE_THETA ** (-jnp.arange(n, dtype=F32) / n)
    t = jnp.arange(tok.ls)
    pos = jnp.stack([(t // GRID_W).astype(F32), (t % GRID_W).astype(F32)], axis=1)
    ang = pos[:, :, None] * inv[None, None, :]
    cos = jnp.cos(ang)
    sin = jnp.sin(ang)
    cos_h = jnp.concatenate([cos, cos], axis=-1).reshape(tok.ls, HEAD_DIM)
    sin_h = jnp.concatenate([-sin, sin], axis=-1).reshape(tok.ls, HEAD_DIM)
    cos_t = jnp.concatenate([jnp.ones((TM, HEAD_DIM), F32), cos_h], axis=0)
    sin_t = jnp.concatenate([jnp.zeros((TM, HEAD_DIM), F32), sin_h], axis=0)
    return jnp.tile(cos_t, (1, 2)), jnp.tile(sin_t, (1, 2))


def _proj_c(tok, x, mod, g_norm, w, g_q, g_k, cos_t, sin_t, layer):
    o = layer // 2
    nk = C_KV_HEADS * HEAD_DIM

    def rope_idx(i):
        return (jnp.where(i < tok.npt, 0, 1 + (i - tok.npt) % tok.tiles_per_seq), 0)

    (qs, qsh), (ks, ksh), (vs, vsh) = (_head_major(C_Q_HEADS, tok), _head_major(C_KV_HEADS, tok),
                                       _head_major(C_KV_HEADS, tok, 2 * HEAD_DIM))
    vec = pl.BlockSpec((None, 1, LANES), lambda i: (o, 0, 0))
    n = g_q.shape[0]
    return pl.pallas_call(
        functools.partial(_proj_c_kernel, npt=tok.npt),
        out_shape=(qsh, ksh, vsh, jax.ShapeDtypeStruct((tok.tp, nk), F32), jax.ShapeDtypeStruct((tok.tp, nk), F32)),
        grid=(tok.nt,),
        in_specs=[tok.x_spec(), tok.mod_spec(layer), _vec_spec(layer, 1),
                  _resident((None,) + w.shape[1:], lambda i: (o, 0, 0)), vec, vec,
                  pl.BlockSpec((TM, LANES), rope_idx), pl.BlockSpec((TM, LANES), rope_idx)],
        out_specs=(qs, ks, vs, tok.prompt_spec(nk), tok.prompt_spec(nk)),
        compiler_params=_cparams("arbitrary"),
        name="proj_c",
    )(x, mod, g_norm, w, jnp.tile(g_q, (1, 2)).reshape(n, 1, LANES), jnp.tile(g_k, (1, 2)).reshape(n, 1, LANES),
      cos_t, sin_t)


def _ctx_heads(cache, with_ones=False):
    t = jnp.transpose(cache, (0, 2, 1, 3))
    if with_ones:
        t = jnp.concatenate([t, jnp.ones_like(t)], axis=-1)
    return t.astype(BF16)


def kernel(x_prompt, x_sample, cache_nb_k, cache_nb_v, cache_attn_k, cache_attn_v, c, c_ctx, w_mod, b_mod, g_norm, w_ffn_in, w_ffn_out, w_in_ab, w_pool, pool_scale, nb_rpb, w_out_ab, w_qkv_c, g_qnorm, g_knorm, w_out_c, g_final):
    bp, lp, d = x_prompt.shape
    bs, ls, _ = x_sample.shape
    assert d == D_MODEL and w_ffn_in.shape[-1] == 2 * D_FF and w_mod.shape[0] == DEPTH
    tok = _Tokens(bp, lp, bs, ls)

    n_rows = SUBLANES * (-(-(1 + bs) // SUBLANES))
    cvec = jnp.concatenate([c_ctx[None], c, jnp.zeros((n_rows - 1 - bs, d), F32)], axis=0)
    mod = _ada(cvec, w_mod, b_mod).reshape(DEPTH, n_rows, N_MOD, d)
    g_norm4 = g_norm.reshape(DEPTH, 3, 1, d)

    w_in_ab_b = w_in_ab.astype(BF16)
    w_pool_b = w_pool.astype(BF16)
    w_out_ab_b = w_out_ab.astype(BF16)
    w_qkv_c_b = w_qkv_c.astype(BF16)
    w_out_c_b = w_out_c.astype(BF16)
    cos_t, sin_t = _rope_tables(tok)

    x = (x_prompt.reshape(tok.tp, d), x_sample.reshape(tok.ts, d))
    nb_k, nb_v, at_k, at_v = [], [], [], []
    for l in range(DEPTH):
        x = _ffn(tok, x, mod, g_norm4, w_ffn_in, w_ffn_out, l, 0)
        if l % 2 == 0:
            e = l // 2
            u, q, k, v, kf, vf = _proj_ab(tok, x, mod, g_norm4, w_in_ab_b, l)
            a = _pool(tok, u, w_pool_b, pool_scale, e)
            b_p = _prompt_attn(tok, q, k, v, NB_HEADS, NB_HEADS)
            rel, rowmask = _nb_bias_tables(nb_rpb[e], ls // GRID_W)
            b_s = _nb_attn(tok, q, k, v, _ctx_heads(cache_nb_k[:, e]), _ctx_heads(cache_nb_v[:, e], True),
                           rel, rowmask)
            mixer = (a, b_p, b_s, w_out_ab_b)
            nb_k.append(kf.reshape(bp, lp, NB_HEADS, HEAD_DIM))
            nb_v.append(vf.reshape(bp, lp, NB_HEADS, HEAD_DIM))
        else:
            o = l // 2
            q, k, v, kf, vf = _proj_c(tok, x, mod, g_norm4, w_qkv_c_b, g_qnorm, g_knorm, cos_t, sin_t, l)
            b_p = _prompt_attn(tok, q, k, v, C_Q_HEADS, C_KV_HEADS)
            b_s = _gqa_attn(tok, q, k, v, _ctx_heads(cache_attn_k[:, o]), _ctx_heads(cache_attn_v[:, o], True))
            mixer = (None, b_p, b_s, w_out_c_b)
            at_k.append(kf.reshape(bp, lp, C_KV_HEADS, HEAD_DIM))
            at_v.append(vf.reshape(bp, lp, C_KV_HEADS, HEAD_DIM))
        x = _ffn(tok, x, mod, g_norm4, w_ffn_in, w_ffn_out, l, 2, g_final=g_final if l == DEPTH - 1 else None,
                 mixer=mixer)

    y_prompt, y_sample = x
    return (y_prompt.reshape(bp, lp, d), y_sample.reshape(bs, ls, d), jnp.stack(nb_k, axis=1),
            jnp.stack(nb_v, axis=1), jnp.stack(at_k, axis=1), jnp.stack(at_v, axis=1))
```

```python
import functools
import math

import numpy as np
import jax
import jax.numpy as jnp
from jax import lax
from jax.experimental import pallas as pl
from jax.experimental.pallas import tpu as pltpu

F32 = jnp.float32
BF16 = jnp.bfloat16

D_MODEL = 1024
DEPTH = 4
GRID_W = 64
D_FF = 2816
N_MOD = 9
A_WIDTH = 512
POOL_WINDOWS = (2, 4, 8, 16)
POOL_GROUP = 128
HEAD_DIM = 64
NB_HEADS = 8
NB_ROWS = 8
NB_COLS = 16
C_Q_HEADS = 16
C_KV_HEADS = 4
ROPE_THETA = 10000.0
EPS = 1e-6
NEG_INF = -1e30
LOG2E = math.log2(math.e)
Q_SCALE = HEAD_DIM ** -0.5 * LOG2E

LANES = 128
SUBLANES = 8
VMEM_LIMIT_BYTES = 56 * 1024 * 1024

TM = 512
TF = 256
TPOOL = 256
POOL_HALO = 8
NB_BAND = 8
NB_KROWS = 16
GQ_TOK = 128
PIPE_UNROLL = 4
KCHUNK = 256


def _cparams(*sem):
    return pltpu.CompilerParams(dimension_semantics=sem, vmem_limit_bytes=VMEM_LIMIT_BYTES)


def _resident(shape, index_map):
    return pl.BlockSpec(shape, index_map, pipeline_mode=pl.Buffered(1))


def _dot(a, b):
    return jnp.dot(a, b, preferred_element_type=F32)


def _dot_nt(a, b):
    return lax.dot_general(a, b, (((1,), (1,)), ((), ())), preferred_element_type=F32)


def _norm_mod(x, g, shift, scale):
    ms = jnp.mean(x * x, axis=-1, keepdims=True)
    y = x * lax.rsqrt(ms + EPS) * g
    return y * (1.0 + scale) + shift


def _with_ones(v):
    return jnp.concatenate([v, jnp.ones_like(v)], axis=-1).astype(BF16)


def _ada_kernel(c_ref, w_ref, b_ref, o_ref):
    c = c_ref[...]
    s = (c * jax.nn.sigmoid(c)).astype(BF16)
    o_ref[...] = _dot(s, w_ref[...].astype(BF16)) + b_ref[...]


def _ada(cvec, w_mod, b_mod):
    rows = cvec.shape[0]
    n = w_mod.shape[-1]
    tn = D_MODEL
    return pl.pallas_call(
        _ada_kernel,
        out_shape=jax.ShapeDtypeStruct((DEPTH, rows, n), F32),
        grid=(DEPTH, n // tn),
        in_specs=[
            pl.BlockSpec((rows, D_MODEL), lambda l, j: (0, 0)),
            pl.BlockSpec((None, D_MODEL, tn), lambda l, j: (l, 0, j)),
            pl.BlockSpec((None, 1, tn), lambda l, j: (l, 0, j)),
        ],
        out_specs=pl.BlockSpec((None, rows, tn), lambda l, j: (l, 0, j)),
        compiler_params=_cparams("arbitrary", "arbitrary"),
        name="ada_mod",
    )(cvec, w_mod, b_mod.reshape(DEPTH, 1, n))


class _Tokens:
    def __init__(self, bp, lp, bs, ls):
        self.bp, self.lp, self.bs, self.ls = bp, lp, bs, ls
        self.tp = bp * lp
        self.ts = bs * ls
        self.t = self.tp + self.ts
        assert self.tp % TM == 0 and ls % TM == 0 and lp % TPOOL == 0 and ls % TPOOL == 0
        self.npt = self.tp // TM
        self.nt = self.t // TM
        self.tiles_per_seq = ls // TM

    def mod_row(self, i):
        return jnp.where(i < self.npt, 0, 1 + (i - self.npt) // self.tiles_per_seq)

    def x_spec(self, width=D_MODEL):
        return pl.BlockSpec((TM, width), lambda i: (i, 0))

    def mod_spec(self, layer):
        return pl.BlockSpec((None, None, N_MOD, D_MODEL), lambda i: (layer, self.mod_row(i), 0, 0))

    def prompt_spec(self, width):
        return pl.BlockSpec((TM, width), lambda i: (jnp.minimum(i, self.npt - 1), 0))

    def sample_spec(self, width):
        return pl.BlockSpec((TM, width), lambda i: (jnp.maximum(i - self.npt, 0), 0))


def _vec_spec(layer, sub):
    return pl.BlockSpec((None, None, 1, D_MODEL), lambda i: (layer, sub, 0, 0))


def _ffn_kernel(*refs, sub, npt, split_in, mix, final):
    refs = list(refs)
    i = pl.program_id(0)
    if split_in:
        xp_ref, xs_ref = refs[:2]
        refs = refs[2:]
        x = jnp.where(i < npt, xp_ref[...], xs_ref[...])
    else:
        x = refs.pop(0)[...]
    if mix is not None:
        a_ref = refs.pop(0) if mix == "pool+attn" else None
        bp_ref, bs_ref, wmix_ref = refs[:3]
        refs = refs[3:]
    mod_ref, g_ref, win_ref, wout_ref = refs[:4]
    refs = refs[4:]
    if final:
        gf_ref, yp_ref, ys_ref = refs
    else:
        (o_ref,) = refs
    if mix is not None:
        b = jnp.where(i < npt, bp_ref[...], bs_ref[...])
        if a_ref is None:
            mixed = _dot(b, wmix_ref[...])
        else:
            mixed = _dot(a_ref[...], wmix_ref[:A_WIDTH, :]) + _dot(b, wmix_ref[A_WIDTH:, :])
        x = x + mod_ref[5:6, :] * mixed
    h = _norm_mod(x, g_ref[...], mod_ref[3 * sub:3 * sub + 1, :], mod_ref[3 * sub + 1:3 * sub + 2, :]).astype(BF16)
    acc = None
    for j in range(D_FF // TF):
        a = _dot(h, win_ref[:, j * TF:(j + 1) * TF].astype(BF16))
        b = _dot(h, win_ref[:, D_FF + j * TF:D_FF + (j + 1) * TF].astype(BF16))
        act = (a * jax.nn.sigmoid(a) * b).astype(BF16)
        t = _dot(act, wout_ref[j * TF:(j + 1) * TF, :].astype(BF16))
        acc = t if acc is None else acc + t
    y = x + mod_ref[3 * sub + 2:3 * sub + 3, :] * (0.5 * acc)
    if final:
        ms = jnp.mean(y * y, axis=-1, keepdims=True)
        y = y * lax.rsqrt(ms + EPS) * gf_ref[...]

        @pl.when(i < npt)
        def _():
            yp_ref[...] = y

        @pl.when(i >= npt)
        def _():
            ys_ref[...] = y
    else:
        o_ref[...] = y


def _ffn(tok, x, mod, g_norm, w_in, w_out, layer, sub, g_final=None, mixer=None):
    split_in = isinstance(x, tuple)
    final = g_final is not None
    which = sub // 2
    mix = None
    if split_in:
        in_specs = [tok.prompt_spec(D_MODEL), tok.sample_spec(D_MODEL)]
        args = list(x)
    else:
        in_specs = [tok.x_spec()]
        args = [x]
    if mixer is not None:
        a, b_prompt, b_sample, w_mix = mixer
        mix = "attn" if a is None else "pool+attn"
        bw = b_prompt.shape[1]
        if a is not None:
            in_specs.append(tok.x_spec(A_WIDTH))
            args.append(a)
        in_specs += [tok.prompt_spec(bw), tok.sample_spec(bw),
                     _resident((None,) + w_mix.shape[1:], lambda i: (layer // 2, 0, 0))]
        args += [b_prompt, b_sample, w_mix]
    in_specs += [
        tok.mod_spec(layer), _vec_spec(layer, sub),
        _resident((None, None, D_MODEL, 2 * D_FF), lambda i: (layer, which, 0, 0)),
        _resident((None, None, D_FF, D_MODEL), lambda i: (layer, which, 0, 0)),
    ]
    args += [mod, g_norm, w_in, w_out]
    if final:
        in_specs.append(pl.BlockSpec((1, D_MODEL), lambda i: (0, 0)))
        args.append(g_final.reshape(1, D_MODEL))
        out_shape = (jax.ShapeDtypeStruct((tok.tp, D_MODEL), F32), jax.ShapeDtypeStruct((tok.ts, D_MODEL), F32))
        out_specs = (tok.prompt_spec(D_MODEL), tok.sample_spec(D_MODEL))
    else:
        out_shape = jax.ShapeDtypeStruct((tok.t, D_MODEL), F32)
        out_specs = tok.x_spec()
    return pl.pallas_call(
        functools.partial(_ffn_kernel, sub=sub, npt=tok.npt, split_in=split_in, mix=mix, final=final),
        out_shape=out_shape,
        grid=(tok.nt,),
        in_specs=in_specs,
        out_specs=out_specs,
        compiler_params=_cparams("arbitrary"),
        name="ffn_final" if final else ("ffn_first" if split_in else ("ffn" if mix is None else "mix_ffn")),
    )(*args)


def _proj_ab_kernel(x_ref, mod_ref, g_ref, w_ref, u_ref, q_ref, k_ref, v_ref, kf_ref, vf_ref, *, npt):
    i = pl.program_id(0)
    h = _norm_mod(x_ref[...], g_ref[...], mod_ref[3:4, :], mod_ref[4:5, :]).astype(BF16)
    proj = _dot(h, w_ref[...])
    u_ref[...] = proj[:, :A_WIDTH]
    hw = NB_HEADS * HEAD_DIM
    for hh in range(NB_HEADS):
        lo = A_WIDTH + hh * HEAD_DIM
        q_ref[hh] = (proj[:, lo:lo + HEAD_DIM] * Q_SCALE).astype(BF16)
        k_ref[hh] = proj[:, lo + hw:lo + hw + HEAD_DIM].astype(BF16)
        v_ref[hh] = _with_ones(proj[:, lo + 2 * hw:lo + 2 * hw + HEAD_DIM])

    @pl.when(i < npt)
    def _():
        kf_ref[...] = proj[:, A_WIDTH + hw:A_WIDTH + 2 * hw]
        vf_ref[...] = proj[:, A_WIDTH + 2 * hw:A_WIDTH + 3 * hw]


def _head_major(n_heads, tok, width=HEAD_DIM):
    return (pl.BlockSpec((n_heads, TM, width), lambda i: (0, i, 0)),
            jax.ShapeDtypeStruct((n_heads, tok.t, width), BF16))


def _proj_ab(tok, x, mod, g_norm, w, layer):
    e = layer // 2
    hw = NB_HEADS * HEAD_DIM
    (qs, qsh), (ks, ksh), (vs, vsh) = (_head_major(NB_HEADS, tok), _head_major(NB_HEADS, tok),
                                       _head_major(NB_HEADS, tok, 2 * HEAD_DIM))
    return pl.pallas_call(
        functools.partial(_proj_ab_kernel, npt=tok.npt),
        out_shape=(jax.ShapeDtypeStruct((tok.t, A_WIDTH), F32), qsh, ksh, vsh,
                   jax.ShapeDtypeStruct((tok.tp, hw), F32), jax.ShapeDtypeStruct((tok.tp, hw), F32)),
        grid=(tok.nt,),
        in_specs=[tok.x_spec(), tok.mod_spec(layer), _vec_spec(layer, 1),
                  _resident((None,) + w.shape[1:], lambda i: (e, 0, 0))],
        out_specs=(tok.x_spec(A_WIDTH), qs, ks, vs, tok.prompt_spec(hw), tok.prompt_spec(hw)),
        compiler_params=_cparams("arbitrary"),
        name="proj_ab",
    )(x, mod, g_norm, w)


def _pool_kernel(u_ref, prev_ref, next_ref, w_ref, sc_ref, o_ref, *, n_prompt_tiles, lp, ls):
    i = pl.program_id(0)
    is_p = i < n_prompt_tiles
    tiles_s = ls // TPOOL
    j = jnp.where(is_p, 0, (i - n_prompt_tiles) % tiles_s)
    seq_len = jnp.where(is_p, lp, ls)
    pos0 = j * TPOOL
    first = j == 0
    last = pos0 + TPOOL == seq_len
    prev = jnp.where(first, 0.0, prev_ref[...])
    nxt = jnp.where(last, 0.0, next_ref[...])
    ext = jnp.concatenate([prev, u_ref[...], nxt], axis=0)
    n_ext = TPOOL + 2 * POOL_HALO
    pos = pos0 + lax.broadcasted_iota(jnp.int32, (TPOOL, 1), 0)

    def fwd(v, d):
        return pltpu.roll(v, n_ext - d, 0)

    for gi, w in enumerate(POOL_WINDOWS):
        e = ext[:, gi * POOL_GROUP:(gi + 1) * POOL_GROUP]
        acc = e
        span = 1
        while span < w:
            acc = acc + fwd(acc, span)
            span *= 2
        half = w // 2
        centred = pltpu.roll(acc, half, 0)
        s = centred[POOL_HALO:POOL_HALO + TPOOL]
        cnt = (jnp.minimum(pos + (w - 1 - half), seq_len - 1) - jnp.maximum(pos - half, 0) + 1).astype(F32)
        pooled = (s / cnt - e[POOL_HALO:POOL_HALO + TPOOL]).astype(BF16)
        mixed = _dot(pooled, w_ref[gi])
        o_ref[:, gi * POOL_GROUP:(gi + 1) * POOL_GROUP] = (
            mixed * sc_ref[:, gi * POOL_GROUP:(gi + 1) * POOL_GROUP]).astype(BF16)


def _pool(tok, u, w_pool, pool_scale, e):
    n_tiles = tok.t // TPOOL
    per = TPOOL // POOL_HALO
    n_halo_blocks = tok.t // POOL_HALO
    return pl.pallas_call(
        functools.partial(_pool_kernel, n_prompt_tiles=tok.tp // TPOOL, lp=tok.lp, ls=tok.ls),
        out_shape=jax.ShapeDtypeStruct((tok.t, A_WIDTH), BF16),
        grid=(n_tiles,),
        in_specs=[
            pl.BlockSpec((TPOOL, A_WIDTH), lambda i: (i, 0)),
            pl.BlockSpec((POOL_HALO, A_WIDTH), lambda i: (jnp.maximum(i * per - 1, 0), 0)),
            pl.BlockSpec((POOL_HALO, A_WIDTH), lambda i: (jnp.minimum((i + 1) * per, n_halo_blocks - 1), 0)),
            pl.BlockSpec((None, len(POOL_WINDOWS), POOL_GROUP, POOL_GROUP), lambda i: (e, 0, 0, 0)),
            pl.BlockSpec((None, 1, A_WIDTH), lambda i: (e, 0, 0)),
        ],
        out_specs=pl.BlockSpec((TPOOL, A_WIDTH), lambda i: (i, 0)),
        compiler_params=_cparams("arbitrary"),
        name="pool_mix",
    )(u, u, u, w_pool, pool_scale.reshape(pool_scale.shape[0], 1, A_WIDTH))


def _normalised(pv, upper):
    swapped = pltpu.roll(pv, HEAD_DIM, 1)
    return swapped / pv if upper else pv / swapped


def _pair_lanes(lower_src, upper_src):
    lane = lax.broadcasted_iota(jnp.int32, lower_src.shape, 1)
    return jnp.where(lane < HEAD_DIM, lower_src, upper_src)


def _attention_pipeline(n_blocks, unroll, q_of, chunks_of, store, s_refs, mx_ref, pv_ref):
    def scores(n, pos, slot):
        q = q_of(n, pos)
        mx = None
        for c, (k_fn, _, bias_fn) in enumerate(chunks_of(n, pos)):
            s = _dot_nt(q, k_fn())
            if bias_fn is not None:
                s = s + bias_fn()
            s_refs[slot][:, c * KCHUNK:(c + 1) * KCHUNK] = s
            part = jnp.maximum(s[:, :LANES], s[:, LANES:])
            mx = part if mx is None else jnp.maximum(mx, part)
        mx_ref[slot] = mx

    def outputs(n, pos, slot):
        mx = mx_ref[slot]
        m = jnp.broadcast_to(mx.max(axis=-1, keepdims=True), mx.shape)
        pv = None
        for c, (_, v_fn, _) in enumerate(chunks_of(n, pos)):
            p = jnp.concatenate(
                [jnp.exp2(s_refs[slot][:, c * KCHUNK + h * LANES:c * KCHUNK + (h + 1) * LANES] - m)
                 for h in range(KCHUNK // LANES)], axis=-1).astype(BF16)
            t = _dot(p, v_fn())
            pv = t if pv is None else pv + t
        pv_ref[slot] = pv

    def finalize(n, pos, slot):
        store(n, pos, pv_ref[slot])

    _run_pipeline(n_blocks, unroll, scores, outputs, finalize, pv_ref)


def _run_pipeline(n_blocks, unroll, scores, outputs, finalize, pv_ref):
    assert n_blocks % unroll == 0 and unroll % 2 == 0
    if n_blocks == unroll:
        scores(0, 0, 0)
        for u in range(unroll):
            if u >= 2:
                finalize(u - 2, u - 2, u % 2)
            outputs(u, u, u % 2)
            if u + 1 < unroll:
                scores(u + 1, u + 1, 1 - u % 2)
        finalize(unroll - 2, unroll - 2, 0)
        finalize(unroll - 1, unroll - 1, 1)
        return

    pv_ref[...] = jnp.ones(pv_ref.shape, F32)
    scores(0, 0, 0)

    def body(j, carry):
        base = unroll * j
        for u in range(unroll):
            slot = u % 2
            pf = (u - 2) % unroll
            finalize(jnp.maximum(base + u - 2, pf), pf, slot)
            outputs(base + u, u, slot)
            ps = (u + 1) % unroll
            scores(jnp.minimum(base + u + 1, n_blocks - unroll + ps), ps, 1 - slot)
        return carry

    lax.fori_loop(0, n_blocks // unroll, body, 0)
    finalize(n_blocks - 2, unroll - 2, 0)
    finalize(n_blocks - 1, unroll - 1, 1)


def _pipeline_scratch(m_rows, n_keys):
    return [pltpu.VMEM((m_rows, n_keys), F32), pltpu.VMEM((m_rows, n_keys), F32),
            pltpu.VMEM((2, m_rows, LANES), F32), pltpu.VMEM((2, m_rows, LANES), F32)]


def _prompt_attn_kernel(q_ref, k_ref, v_ref, o_ref, s0_ref, s1_ref, mx_ref, pv_ref, *, group):
    n_kv = k_ref.shape[0]
    seq = q_ref.shape[1]

    def q_of(n, pos):
        return jnp.concatenate([q_ref[group * pos + g] for g in range(group)], axis=0) if group > 1 else q_ref[pos]

    def chunks_of(n, pos):
        return [(functools.partial(lambda c0: k_ref[pos, c0:c0 + KCHUNK, :], c0),
                 functools.partial(lambda c0: v_ref[pos, c0:c0 + KCHUNK, :], c0), None)
                for c0 in range(0, seq, KCHUNK)]

    def store(n, pos, pv):
        if group % 2 == 0:
            lower, upper = _normalised(pv, False), _normalised(pv, True)
            for g in range(0, group, 2):
                col = (group * pos + g) * HEAD_DIM
                o_ref[:, col:col + 2 * HEAD_DIM] = _pair_lanes(
                    lower[g * seq:(g + 1) * seq], upper[(g + 1) * seq:(g + 2) * seq]).astype(BF16)
        else:
            half = (pos % 2) * HEAD_DIM
            o_ref[:, pos * HEAD_DIM:(pos + 1) * HEAD_DIM] = (
                _normalised(pv, pos % 2 == 1)[:, half:half + HEAD_DIM].astype(BF16))

    _attention_pipeline(n_kv, n_kv, q_of, chunks_of, store, (s0_ref, s1_ref), mx_ref, pv_ref)


def _prompt_attn(tok, q, k, v, n_q_heads, n_kv_heads):
    group = n_q_heads // n_kv_heads
    assert tok.lp % KCHUNK == 0 and n_kv_heads % 2 == 0
    return pl.pallas_call(
        functools.partial(_prompt_attn_kernel, group=group),
        out_shape=jax.ShapeDtypeStruct((tok.tp, n_q_heads * HEAD_DIM), BF16),
        grid=(tok.bp,),
        in_specs=[
            pl.BlockSpec((n_q_heads, tok.lp, HEAD_DIM), lambda b: (0, b, 0)),
            pl.BlockSpec((n_kv_heads, tok.lp, HEAD_DIM), lambda b: (0, b, 0)),
            pl.BlockSpec((n_kv_heads, tok.lp, 2 * HEAD_DIM), lambda b: (0, b, 0)),
        ],
        out_specs=pl.BlockSpec((tok.lp, n_q_heads * HEAD_DIM), lambda b: (b, 0)),
        scratch_shapes=_pipeline_scratch(group * tok.lp, tok.lp),
        compiler_params=_cparams("arbitrary"),
        name="prompt_attn",
    )(q, k, v)


def _nb_attn_kernel(q_ref, k_ref, v_ref, ck_ref, cv_ref, rel_ref, rowmask_ref, o_ref,
                    s0_ref, s1_ref, mx_ref, pv_ref, *, rows, n_ctx):
    n_bands = rows // NB_BAND
    tq = NB_BAND * GRID_W

    def place(n, pos):
        return pos // 2, 2 * (n // PIPE_UNROLL) + pos % 2

    def q_of(n, pos):
        hh, band = place(n, pos)
        return q_ref[hh, pl.ds(pl.multiple_of(band * tq, tq), tq), :]

    def chunks_of(n, pos):
        hh, band = place(n, pos)
        ty = jnp.where(band == 0, 0, jnp.where(band == n_bands - 1, 2, 1))
        row0 = jnp.clip(NB_BAND * band - NB_ROWS // 2, 0, rows - NB_KROWS)
        start = row0 * GRID_W

        def bias(c0):
            strips = []
            for j in range(NB_BAND):
                off = (NB_BAND - 1 - j) * GRID_W
                copy = (off // GRID_W) % 2
                lo = off - copy * GRID_W + c0
                strips.append(rel_ref[ty, hh, copy, :, lo:lo + KCHUNK])
            return jnp.concatenate(strips, axis=0) + rowmask_ref[ty, :, c0:c0 + KCHUNK]

        def band_rows(ref, c0):
            return ref[hh, pl.ds(pl.multiple_of(start + c0, KCHUNK), KCHUNK), :]

        out = [(functools.partial(band_rows, k_ref, c0), functools.partial(band_rows, v_ref, c0),
                functools.partial(bias, c0)) for c0 in range(0, NB_KROWS * GRID_W, KCHUNK)]
        out += [(functools.partial(lambda c0: ck_ref[hh, c0:c0 + KCHUNK, :], c0),
                 functools.partial(lambda c0: cv_ref[hh, c0:c0 + KCHUNK, :], c0), None)
                for c0 in range(0, n_ctx, KCHUNK)]
        return out

    def store(n, pos, pv):
        hh, band = place(n, pos)
        half = hh * HEAD_DIM
        o_ref[pl.ds(pl.multiple_of(band * tq, tq), tq), half:half + HEAD_DIM] = (
            _normalised(pv, hh == 1)[:, half:half + HEAD_DIM].astype(BF16))

    _attention_pipeline(2 * n_bands, PIPE_UNROLL, q_of, chunks_of, store, (s0_ref, s1_ref), mx_ref, pv_ref)


def _nb_attn(tok, q, k, v, ck, cv, rel, rowmask, e):
    rows = tok.ls // GRID_W
    tq = NB_BAND * GRID_W
    kv0 = tok.tp // tok.ls
    past = ck.shape[2]
    assert tok.tp % tok.ls == 0 and past % KCHUNK == 0 and (rows // NB_BAND) % 2 == 0
    return pl.pallas_call(
        functools.partial(_nb_attn_kernel, rows=rows, n_ctx=past),
        out_shape=jax.ShapeDtypeStruct((tok.ts, NB_HEADS * HEAD_DIM), BF16),
        grid=(tok.bs, NB_HEADS // 2),
        in_specs=[
            pl.BlockSpec((2, tok.ls, HEAD_DIM), lambda b, p: (p, kv0 + b, 0)),
            pl.BlockSpec((2, tok.ls, HEAD_DIM), lambda b, p: (p, kv0 + b, 0)),
            pl.BlockSpec((2, tok.ls, 2 * HEAD_DIM), lambda b, p: (p, kv0 + b, 0)),
            pl.BlockSpec((None, 2, past, HEAD_DIM), lambda b, p: (b, p, 0, 0)),
            pl.BlockSpec((None, 2, past, 2 * HEAD_DIM), lambda b, p: (b, p, 0, 0)),
            pl.BlockSpec((None, 3, 2, 2, GRID_W, rel.shape[-1]), lambda b, p: (e, 0, p, 0, 0, 0)),
            _resident(rowmask.shape, lambda b, p: (0, 0, 0)),
        ],
        out_specs=pl.BlockSpec((tok.ls, 2 * HEAD_DIM), lambda b, p: (b, p)),
        scratch_shapes=_pipeline_scratch(tq, NB_KROWS * GRID_W + past),
        compiler_params=_cparams("arbitrary", "arbitrary"),
        name="nb_attn",
    )(q, k, v, ck, cv, rel, rowmask)


def _nb_band(band, rows):
    n_bands = rows // NB_BAND
    ty = 0 if band == 0 else (2 if band == n_bands - 1 else 1)
    row0 = int(np.clip(NB_BAND * band - NB_ROWS // 2, 0, rows - NB_KROWS))
    return ty, row0, row0 - NB_BAND * band + NB_ROWS - 1


def _nb_row_valid(rows):
    n_bands = rows // NB_BAND
    valid = np.zeros((3, NB_BAND, NB_KROWS), bool)
    for band in (0, 1, n_bands - 1):
        ty, row0, _ = _nb_band(band, rows)
        for jq in range(NB_BAND):
            r = NB_BAND * band + jq
            rs = int(np.clip(r - NB_ROWS // 2, 0, rows - NB_ROWS))
            for m in range(NB_KROWS):
                valid[ty, jq, m] = rs <= row0 + m < rs + NB_ROWS
    return valid


def _toeplitz_cols(rpb):
    w = GRID_W
    n_edge = w - NB_COLS
    lead = rpb.shape[:-1]
    e = jnp.concatenate([jnp.broadcast_to(rpb[..., :1], lead + (n_edge,)), rpb,
                         jnp.broadcast_to(rpb[..., -1:], lead + (n_edge + 1,))], axis=-1)
    flat = jnp.tile(e, (1,) * len(lead) + (w,))[..., :w * (2 * w - 1)]
    return flat.reshape(lead + (w, 2 * w - 1))[..., w - 1:]


def _nb_bias_tables(rpb, rows):
    n_layers, n_heads = rpb.shape[:2]
    n_bands = rows // NB_BAND
    cq = np.arange(GRID_W)
    cstart = np.clip(cq - NB_COLS // 2, 0, GRID_W - NB_COLS)
    col_valid = (cq[None, :] >= cstart[:, None]) & (cq[None, :] < cstart[:, None] + NB_COLS)
    blocks = jnp.where(col_valid, _toeplitz_cols(rpb) * LOG2E, NEG_INF)
    n_d = 2 * NB_ROWS - 1
    blocks = jnp.concatenate([blocks, jnp.zeros((n_layers, n_heads, 1, GRID_W, GRID_W), F32)], axis=2)
    offs = np.array([_nb_band(band, rows)[2] for band in (0, 1, n_bands - 1)])
    n_strip = NB_BAND + NB_KROWS
    d = (offs[:, None, None] - (NB_BAND - 1) + np.arange(2)[None, :, None]
         + np.arange(n_strip)[None, None, :])
    d = np.where((d >= 0) & (d < n_d), d, n_d)
    t = jnp.take(blocks, jnp.asarray(d.reshape(-1)), axis=2)
    t = t.reshape(n_layers, n_heads, 3, 2, n_strip, GRID_W, GRID_W)
    rel = jnp.transpose(t, (0, 2, 1, 3, 5, 4, 6)).reshape(n_layers, 3, n_heads, 2, GRID_W, n_strip * GRID_W)
    small = jnp.where(jnp.asarray(_nb_row_valid(rows)), 0.0, NEG_INF).astype(F32)
    rowmask = jnp.broadcast_to(small[:, :, None, :, None], (3, NB_BAND, GRID_W, NB_KROWS, GRID_W))
    return rel, rowmask.reshape(3, NB_BAND * GRID_W, NB_KROWS * GRID_W)


def _gqa_attn_kernel(q_ref, k_ref, v_ref, ck_ref, cv_ref, o_ref, s0_ref, s1_ref, mx_ref, pv_ref,
                     *, group, n_ctx, n_self):
    def q_of(n, pos):
        r0 = pl.multiple_of(n * GQ_TOK, GQ_TOK)
        return jnp.concatenate([q_ref[g, pl.ds(r0, GQ_TOK), :] for g in range(group)], axis=0)

    def chunks_of(n, pos):
        out = [(functools.partial(lambda c0: ck_ref[0, c0:c0 + KCHUNK, :], c0),
                functools.partial(lambda c0: cv_ref[0, c0:c0 + KCHUNK, :], c0), None)
               for c0 in range(0, n_ctx, KCHUNK)]
        out += [(functools.partial(lambda c0: k_ref[0, c0:c0 + KCHUNK, :], c0),
                 functools.partial(lambda c0: v_ref[0, c0:c0 + KCHUNK, :], c0), None)
                for c0 in range(0, n_self, KCHUNK)]
        return out

    def store(n, pos, pv):
        r0 = pl.multiple_of(n * GQ_TOK, GQ_TOK)
        lower, upper = _normalised(pv, False), _normalised(pv, True)
        pairs = [_pair_lanes(lower[g * GQ_TOK:(g + 1) * GQ_TOK], upper[(g + 1) * GQ_TOK:(g + 2) * GQ_TOK])
                 for g in range(0, group, 2)]
        o_ref[pl.ds(r0, GQ_TOK), :] = jnp.concatenate(pairs, axis=-1).astype(BF16)

    _attention_pipeline(q_ref.shape[1] // GQ_TOK, PIPE_UNROLL, q_of, chunks_of, store,
                        (s0_ref, s1_ref), mx_ref, pv_ref)


def _gqa_attn(tok, q, k, v, ck, cv):
    group = C_Q_HEADS // C_KV_HEADS
    kv0 = tok.tp // tok.ls
    past = ck.shape[2]
    assert past % KCHUNK == 0 and tok.ls % KCHUNK == 0 and tok.ls % (GQ_TOK * PIPE_UNROLL) == 0 and group % 2 == 0
    return pl.pallas_call(
        functools.partial(_gqa_attn_kernel, group=group, n_ctx=past, n_self=tok.ls),
        out_shape=jax.ShapeDtypeStruct((tok.ts, C_Q_HEADS * HEAD_DIM), BF16),
        grid=(tok.bs, C_KV_HEADS),
        in_specs=[
            pl.BlockSpec((group, tok.ls, HEAD_DIM), lambda b, g: (g, kv0 + b, 0)),
            pl.BlockSpec((1, tok.ls, HEAD_DIM), lambda b, g: (g, kv0 + b, 0)),
            pl.BlockSpec((1, tok.ls, 2 * HEAD_DIM), lambda b, g: (g, kv0 + b, 0)),
            pl.BlockSpec((None, 1, past, HEAD_DIM), lambda b, g: (b, g, 0, 0)),
            pl.BlockSpec((None, 1, past, 2 * HEAD_DIM), lambda b, g: (b, g, 0, 0)),
        ],
        out_specs=pl.BlockSpec((tok.ls, group * HEAD_DIM), lambda b, g: (b, g)),
        scratch_shapes=_pipeline_scratch(group * GQ_TOK, past + tok.ls),
        compiler_params=_cparams("arbitrary", "arbitrary"),
        name="gqa_attn",
    )(q, k, v, ck, cv)


def _partner(x, s, lane):
    return jnp.where((lane & s) != 0, pltpu.roll(x, s, 1), pltpu.roll(x, LANES - s, 1))


def _head_rms(x, g, seg_ones):
    y = x * x
    hi = y.astype(BF16)
    lo = (y - hi.astype(F32)).astype(BF16)
    ss = _dot(hi, seg_ones) + _dot(lo, seg_ones)
    return x * lax.rsqrt(ss * (1.0 / HEAD_DIM) + EPS) * g


def _rope(x, cos, sin, lane):
    return x * cos + _partner(x, HEAD_DIM // 4, lane) * sin


def _proj_c_kernel(x_ref, mod_ref, g_ref, w_ref, gq_ref, gk_ref, cos_ref, sin_ref,
                   q_ref, k_ref, v_ref, kf_ref, vf_ref, *, npt):
    i = pl.program_id(0)
    h = _norm_mod(x_ref[...], g_ref[...], mod_ref[3:4, :], mod_ref[4:5, :]).astype(BF16)
    proj = _dot(h, w_ref[...])
    nq = C_Q_HEADS * HEAD_DIM
    nk = C_KV_HEADS * HEAD_DIM
    lane = lax.broadcasted_iota(jnp.int32, (TM, LANES), 1)
    seg_ones = (lax.broadcasted_iota(jnp.int32, (LANES, LANES), 0) // HEAD_DIM ==
                lax.broadcasted_iota(jnp.int32, (LANES, LANES), 1) // HEAD_DIM).astype(BF16)
    cos = cos_ref[...]
    sin = sin_ref[...]
    gq = gq_ref[...]
    gk = gk_ref[...]
    for c in range(nq // LANES):
        y = _head_rms(proj[:, c * LANES:(c + 1) * LANES], gq, seg_ones)
        y = _rope(y, cos, sin, lane) * Q_SCALE
        q_ref[2 * c] = y[:, :HEAD_DIM].astype(BF16)
        q_ref[2 * c + 1] = y[:, HEAD_DIM:].astype(BF16)
    for c in range(nk // LANES):
        yk = _head_rms(proj[:, nq + c * LANES:nq + (c + 1) * LANES], gk, seg_ones)

        @pl.when(i < npt)
        def _(yk=yk, c=c):
            kf_ref[:, c * LANES:(c + 1) * LANES] = yk

        yr = _rope(yk, cos, sin, lane)
        k_ref[2 * c] = yr[:, :HEAD_DIM].astype(BF16)
        k_ref[2 * c + 1] = yr[:, HEAD_DIM:].astype(BF16)
    for hh in range(C_KV_HEADS):
        v_ref[hh] = _with_ones(proj[:, nq + nk + hh * HEAD_DIM:nq + nk + (hh + 1) * HEAD_DIM])

    @pl.when(i < npt)
    def _():
        vf_ref[...] = proj[:, nq + nk:nq + 2 * nk]


def _rope_tables(tok):
    n = HEAD_DIM // 4
    inv = ROPE_THETA ** (-jnp.arange(n, dtype=F32) / n)
    t = jnp.arange(tok.ls)
    pos = jnp.stack([(t // GRID_W).astype(F32), (t % GRID_W).astype(F32)], axis=1)
    ang = pos[:, :, None] * inv[None, None, :]
    cos = jnp.cos(ang)
    sin = jnp.sin(ang)
    cos_h = jnp.concatenate([cos, cos], axis=-1).reshape(tok.ls, HEAD_DIM)
    sin_h = jnp.concatenate([-sin, sin], axis=-1).reshape(tok.ls, HEAD_DIM)
    cos_t = jnp.concatenate([jnp.ones((TM, HEAD_DIM), F32), cos_h], axis=0)
    sin_t = jnp.concatenate([jnp.zeros((TM, HEAD_DIM), F32), sin_h], axis=0)
    return jnp.tile(cos_t, (1, 2)), jnp.tile(sin_t, (1, 2))


def _proj_c(tok, x, mod, g_norm, w, g_q, g_k, cos_t, sin_t, layer):
    o = layer // 2
    nk = C_KV_HEADS * HEAD_DIM

    def rope_idx(i):
        return (jnp.where(i < tok.npt, 0, 1 + (i - tok.npt) % tok.tiles_per_seq), 0)

    (qs, qsh), (ks, ksh), (vs, vsh) = (_head_major(C_Q_HEADS, tok), _head_major(C_KV_HEADS, tok),
                                       _head_major(C_KV_HEADS, tok, 2 * HEAD_DIM))
    vec = pl.BlockSpec((None, 1, LANES), lambda i: (o, 0, 0))
    n = g_q.shape[0]
    return pl.pallas_call(
        functools.partial(_proj_c_kernel, npt=tok.npt),
        out_shape=(qsh, ksh, vsh, jax.ShapeDtypeStruct((tok.tp, nk), F32), jax.ShapeDtypeStruct((tok.tp, nk), F32)),
        grid=(tok.nt,),
        in_specs=[tok.x_spec(), tok.mod_spec(layer), _vec_spec(layer, 1),
                  _resident((None,) + w.shape[1:], lambda i: (o, 0, 0)), vec, vec,
                  pl.BlockSpec((TM, LANES), rope_idx), pl.BlockSpec((TM, LANES), rope_idx)],
        out_specs=(qs, ks, vs, tok.prompt_spec(nk), tok.prompt_spec(nk)),
        compiler_params=_cparams("arbitrary"),
        name="proj_c",
    )(x, mod, g_norm, w, jnp.tile(g_q, (1, 2)).reshape(n, 1, LANES), jnp.tile(g_k, (1, 2)).reshape(n, 1, LANES),
      cos_t, sin_t)


def _ctx_heads(cache, with_ones=False):
    t = jnp.transpose(cache, (0, 2, 1, 3))
    if with_ones:
        t = jnp.concatenate([t, jnp.ones_like(t)], axis=-1)
    return t.astype(BF16)


def kernel(x_prompt, x_sample, cache_nb_k, cache_nb_v, cache_attn_k, cache_attn_v, c, c_ctx, w_mod, b_mod, g_norm, w_ffn_in, w_ffn_out, w_in_ab, w_pool, pool_scale, nb_rpb, w_out_ab, w_qkv_c, g_qnorm, g_knorm, w_out_c, g_final):
    bp, lp, d = x_prompt.shape
    bs, ls, _ = x_sample.shape
    assert d == D_MODEL and w_ffn_in.shape[-1] == 2 * D_FF and w_mod.shape[0] == DEPTH
    tok = _Tokens(bp, lp, bs, ls)

    n_rows = SUBLANES * (-(-(1 + bs) // SUBLANES))
    cvec = jnp.concatenate([c_ctx[None], c, jnp.zeros((n_rows - 1 - bs, d), F32)], axis=0)
    mod = _ada(cvec, w_mod, b_mod).reshape(DEPTH, n_rows, N_MOD, d)
    g_norm4 = g_norm.reshape(DEPTH, 3, 1, d)

    w_in_ab_b = w_in_ab.astype(BF16)
    w_pool_b = w_pool.astype(BF16)
    w_out_ab_b = w_out_ab.astype(BF16)
    w_qkv_c_b = w_qkv_c.astype(BF16)
    w_out_c_b = w_out_c.astype(BF16)
    cos_t, sin_t = _rope_tables(tok)
    rel, rowmask = _nb_bias_tables(nb_rpb, ls // GRID_W)

    x = (x_prompt.reshape(tok.tp, d), x_sample.reshape(tok.ts, d))
    nb_k, nb_v, at_k, at_v = [], [], [], []
    for l in range(DEPTH):
        x = _ffn(tok, x, mod, g_norm4, w_ffn_in, w_ffn_out, l, 0)
        if l % 2 == 0:
            e = l // 2
            u, q, k, v, kf, vf = _proj_ab(tok, x, mod, g_norm4, w_in_ab_b, l)
            a = _pool(tok, u, w_pool_b, pool_scale, e)
            b_p = _prompt_attn(tok, q, k, v, NB_HEADS, NB_HEADS)
            b_s = _nb_attn(tok, q, k, v, _ctx_heads(cache_nb_k[:, e]), _ctx_heads(cache_nb_v[:, e], True),
                           rel, rowmask, e)
            mixer = (a, b_p, b_s, w_out_ab_b)
            nb_k.append(kf.reshape(bp, lp, NB_HEADS, HEAD_DIM))
            nb_v.append(vf.reshape(bp, lp, NB_HEADS, HEAD_DIM))
        else:
            o = l // 2
            q, k, v, kf, vf = _proj_c(tok, x, mod, g_norm4, w_qkv_c_b, g_qnorm, g_knorm, cos_t, sin_t, l)
            b_p = _prompt_attn(tok, q, k, v, C_Q_HEADS, C_KV_HEADS)
            b_s = _gqa_attn(tok, q, k, v, _ctx_heads(cache_attn_k[:, o]), _ctx_heads(cache_attn_v[:, o], True))
            mixer = (None, b_p, b_s, w_out_c_b)
            at_k.append(kf.reshape(bp, lp, C_KV_HEADS, HEAD_DIM))
            at_v.append(vf.reshape(bp, lp, C_KV_HEADS, HEAD_DIM))
        x = _ffn(tok, x, mod, g_norm4, w_ffn_in, w_ffn_out, l, 2, g_final=g_final if l == DEPTH - 1 else None,
                 mixer=mixer)

    y_prompt, y_sample = x
    return (y_prompt.reshape(bp, lp, d), y_sample.reshape(bs, ls, d), jnp.stack(nb_k, axis=1),
            jnp.stack(nb_v, axis=1), jnp.stack(at_k, axis=1), jnp.stack(at_v, axis=1))
```

```python
import functools
import math

import numpy as np
import jax
import jax.numpy as jnp
from jax import lax
from jax.experimental import pallas as pl
from jax.experimental.pallas import tpu as pltpu

F32 = jnp.float32
BF16 = jnp.bfloat16

D_MODEL = 1024
DEPTH = 4
GRID_W = 64
D_FF = 2816
N_MOD = 9
A_WIDTH = 512
POOL_WINDOWS = (2, 4, 8, 16)
POOL_GROUP = 128
HEAD_DIM = 64
NB_HEADS = 8
NB_ROWS = 8
NB_COLS = 16
C_Q_HEADS = 16
C_KV_HEADS = 4
ROPE_THETA = 10000.0
EPS = 1e-6
NEG_INF = -1e30
LOG2E = math.log2(math.e)
Q_SCALE = HEAD_DIM ** -0.5 * LOG2E

LANES = 128
SUBLANES = 8
VMEM_LIMIT_BYTES = 56 * 1024 * 1024

TM = 512
TF = 256
TPOOL = 256
POOL_HALO = 8
NB_BAND = 8
NB_KROWS = 16
GQ_TOK = 128
PIPE_UNROLL = 4
KCHUNK = 256


def _cparams(*sem):
    return pltpu.CompilerParams(dimension_semantics=sem, vmem_limit_bytes=VMEM_LIMIT_BYTES)


def _resident(shape, index_map):
    return pl.BlockSpec(shape, index_map, pipeline_mode=pl.Buffered(1))


def _dot(a, b):
    return jnp.dot(a, b, preferred_element_type=F32)


def _dot_nt(a, b):
    return lax.dot_general(a, b, (((1,), (1,)), ((), ())), preferred_element_type=F32)


def _norm_mod(x, g, shift, scale):
    ms = jnp.mean(x * x, axis=-1, keepdims=True)
    y = x * lax.rsqrt(ms + EPS) * g
    return y * (1.0 + scale) + shift


def _with_ones(v):
    return jnp.concatenate([v, jnp.ones_like(v)], axis=-1).astype(BF16)


def _ada_kernel(c_ref, w_ref, b_ref, o_ref):
    c = c_ref[...]
    s = (c * jax.nn.sigmoid(c)).astype(BF16)
    o_ref[...] = _dot(s, w_ref[...].astype(BF16)) + b_ref[...]


def _ada(cvec, w_mod, b_mod):
    rows = cvec.shape[0]
    n = w_mod.shape[-1]
    tn = D_MODEL
    return pl.pallas_call(
        _ada_kernel,
        out_shape=jax.ShapeDtypeStruct((DEPTH, rows, n), F32),
        grid=(DEPTH, n // tn),
        in_specs=[
            pl.BlockSpec((rows, D_MODEL), lambda l, j: (0, 0)),
            pl.BlockSpec((None, D_MODEL, tn), lambda l, j: (l, 0, j)),
            pl.BlockSpec((None, 1, tn), lambda l, j: (l, 0, j)),
        ],
        out_specs=pl.BlockSpec((None, rows, tn), lambda l, j: (l, 0, j)),
        compiler_params=_cparams("arbitrary", "arbitrary"),
        name="ada_mod",
    )(cvec, w_mod, b_mod.reshape(DEPTH, 1, n))


class _Tokens:
    def __init__(self, bp, lp, bs, ls):
        self.bp, self.lp, self.bs, self.ls = bp, lp, bs, ls
        self.tp = bp * lp
        self.ts = bs * ls
        self.t = self.tp + self.ts
        assert self.tp % TM == 0 and ls % TM == 0 and lp % TPOOL == 0 and ls % TPOOL == 0
        self.npt = self.tp // TM
        self.nt = self.t // TM
        self.tiles_per_seq = ls // TM

    def mod_row(self, i):
        return jnp.where(i < self.npt, 0, 1 + (i - self.npt) // self.tiles_per_seq)

    def x_spec(self, width=D_MODEL):
        return pl.BlockSpec((TM, width), lambda i: (i, 0))

    def mod_spec(self, layer):
        return pl.BlockSpec((None, None, N_MOD, D_MODEL), lambda i: (layer, self.mod_row(i), 0, 0))

    def prompt_spec(self, width):
        return pl.BlockSpec((TM, width), lambda i: (jnp.minimum(i, self.npt - 1), 0))

    def sample_spec(self, width):
        return pl.BlockSpec((TM, width), lambda i: (jnp.maximum(i - self.npt, 0), 0))


def _vec_spec(layer, sub):
    return pl.BlockSpec((None, None, 1, D_MODEL), lambda i: (layer, sub, 0, 0))


def _ffn_kernel(*refs, sub, npt, split_in, mix, final):
    refs = list(refs)
    i = pl.program_id(0)
    if split_in:
        xp_ref, xs_ref = refs[:2]
        refs = refs[2:]
        x = jnp.where(i < npt, xp_ref[...], xs_ref[...])
    else:
        x = refs.pop(0)[...]
    if mix is not None:
        a_ref = refs.pop(0) if mix == "pool+attn" else None
        bp_ref, bs_ref, wmix_ref = refs[:3]
        refs = refs[3:]
    mod_ref, g_ref, win_ref, wout_ref = refs[:4]
    refs = refs[4:]
    if final:
        gf_ref, yp_ref, ys_ref = refs
    else:
        (o_ref,) = refs
    if mix is not None:
        b = jnp.where(i < npt, bp_ref[...], bs_ref[...])
        if a_ref is None:
            mixed = _dot(b, wmix_ref[...])
        else:
            mixed = _dot(a_ref[...], wmix_ref[:A_WIDTH, :]) + _dot(b, wmix_ref[A_WIDTH:, :])
        x = x + mod_ref[5:6, :] * mixed
    h = _norm_mod(x, g_ref[...], mod_ref[3 * sub:3 * sub + 1, :], mod_ref[3 * sub + 1:3 * sub + 2, :]).astype(BF16)
    acc = None
    for j in range(D_FF // TF):
        a = _dot(h, win_ref[:, j * TF:(j + 1) * TF].astype(BF16))
        b = _dot(h, win_ref[:, D_FF + j * TF:D_FF + (j + 1) * TF].astype(BF16))
        act = (a * jax.nn.sigmoid(a) * b).astype(BF16)
        t = _dot(act, wout_ref[j * TF:(j + 1) * TF, :].astype(BF16))
        acc = t if acc is None else acc + t
    y = x + mod_ref[3 * sub + 2:3 * sub + 3, :] * (0.5 * acc)
    if final:
        ms = jnp.mean(y * y, axis=-1, keepdims=True)
        y = y * lax.rsqrt(ms + EPS) * gf_ref[...]

        @pl.when(i < npt)
        def _():
            yp_ref[...] = y

        @pl.when(i >= npt)
        def _():
            ys_ref[...] = y
    else:
        o_ref[...] = y


def _ffn(tok, x, mod, g_norm, w_in, w_out, layer, sub, g_final=None, mixer=None):
    split_in = isinstance(x, tuple)
    final = g_final is not None
    which = sub // 2
    mix = None
    if split_in:
        in_specs = [tok.prompt_spec(D_MODEL), tok.sample_spec(D_MODEL)]
        args = list(x)
    else:
        in_specs = [tok.x_spec()]
        args = [x]
    if mixer is not None:
        a, b_prompt, b_sample, w_mix = mixer
        mix = "attn" if a is None else "pool+attn"
        bw = b_prompt.shape[1]
        if a is not None:
            in_specs.append(tok.x_spec(A_WIDTH))
            args.append(a)
        in_specs += [tok.prompt_spec(bw), tok.sample_spec(bw),
                     _resident((None,) + w_mix.shape[1:], lambda i: (layer // 2, 0, 0))]
        args += [b_prompt, b_sample, w_mix]
    in_specs += [
        tok.mod_spec(layer), _vec_spec(layer, sub),
        _resident((None, None, D_MODEL, 2 * D_FF), lambda i: (layer, which, 0, 0)),
        _resident((None, None, D_FF, D_MODEL), lambda i: (layer, which, 0, 0)),
    ]
    args += [mod, g_norm, w_in, w_out]
    if final:
        in_specs.append(pl.BlockSpec((1, D_MODEL), lambda i: (0, 0)))
        args.append(g_final.reshape(1, D_MODEL))
        out_shape = (jax.ShapeDtypeStruct((tok.tp, D_MODEL), F32), jax.ShapeDtypeStruct((tok.ts, D_MODEL), F32))
        out_specs = (tok.prompt_spec(D_MODEL), tok.sample_spec(D_MODEL))
    else:
        out_shape = jax.ShapeDtypeStruct((tok.t, D_MODEL), F32)
        out_specs = tok.x_spec()
    return pl.pallas_call(
        functools.partial(_ffn_kernel, sub=sub, npt=tok.npt, split_in=split_in, mix=mix, final=final),
        out_shape=out_shape,
        grid=(tok.nt,),
        in_specs=in_specs,
        out_specs=out_specs,
        compiler_params=_cparams("arbitrary"),
        name="ffn_final" if final else ("ffn_first" if split_in else ("ffn" if mix is None else "mix_ffn")),
    )(*args)


def _proj_ab_kernel(x_ref, mod_ref, g_ref, w_ref, u_ref, q_ref, k_ref, v_ref, kf_ref, vf_ref, *, npt):
    i = pl.program_id(0)
    h = _norm_mod(x_ref[...], g_ref[...], mod_ref[3:4, :], mod_ref[4:5, :]).astype(BF16)
    proj = _dot(h, w_ref[...])
    u_ref[...] = proj[:, :A_WIDTH]
    hw = NB_HEADS * HEAD_DIM
    for hh in range(NB_HEADS):
        lo = A_WIDTH + hh * HEAD_DIM
        q_ref[hh] = (proj[:, lo:lo + HEAD_DIM] * Q_SCALE).astype(BF16)
        k_ref[hh] = proj[:, lo + hw:lo + hw + HEAD_DIM].astype(BF16)
        v_ref[hh] = _with_ones(proj[:, lo + 2 * hw:lo + 2 * hw + HEAD_DIM])

    @pl.when(i < npt)
    def _():
        kf_ref[...] = proj[:, A_WIDTH + hw:A_WIDTH + 2 * hw]
        vf_ref[...] = proj[:, A_WIDTH + 2 * hw:A_WIDTH + 3 * hw]


def _head_major(n_heads, tok, width=HEAD_DIM):
    return (pl.BlockSpec((n_heads, TM, width), lambda i: (0, i, 0)),
            jax.ShapeDtypeStruct((n_heads, tok.t, width), BF16))


def _proj_ab(tok, x, mod, g_norm, w, layer):
    e = layer // 2
    hw = NB_HEADS * HEAD_DIM
    (qs, qsh), (ks, ksh), (vs, vsh) = (_head_major(NB_HEADS, tok), _head_major(NB_HEADS, tok),
                                       _head_major(NB_HEADS, tok, 2 * HEAD_DIM))
    return pl.pallas_call(
        functools.partial(_proj_ab_kernel, npt=tok.npt),
        out_shape=(jax.ShapeDtypeStruct((tok.t, A_WIDTH), F32), qsh, ksh, vsh,
                   jax.ShapeDtypeStruct((tok.tp, hw), F32), jax.ShapeDtypeStruct((tok.tp, hw), F32)),
        grid=(tok.nt,),
        in_specs=[tok.x_spec(), tok.mod_spec(layer), _vec_spec(layer, 1),
                  _resident((None,) + w.shape[1:], lambda i: (e, 0, 0))],
        out_specs=(tok.x_spec(A_WIDTH), qs, ks, vs, tok.prompt_spec(hw), tok.prompt_spec(hw)),
        compiler_params=_cparams("arbitrary"),
        name="proj_ab",
    )(x, mod, g_norm, w)


def _pool_kernel(u_ref, prev_ref, next_ref, w_ref, sc_ref, o_ref, *, n_prompt_tiles, lp, ls):
    i = pl.program_id(0)
    is_p = i < n_prompt_tiles
    tiles_s = ls // TPOOL
    j = jnp.where(is_p, 0, (i - n_prompt_tiles) % tiles_s)
    seq_len = jnp.where(is_p, lp, ls)
    pos0 = j * TPOOL
    first = j == 0
    last = pos0 + TPOOL == seq_len
    prev = jnp.where(first, 0.0, prev_ref[...])
    nxt = jnp.where(last, 0.0, next_ref[...])
    ext = jnp.concatenate([prev, u_ref[...], nxt], axis=0)
    n_ext = TPOOL + 2 * POOL_HALO
    pos = pos0 + lax.broadcasted_iota(jnp.int32, (TPOOL, 1), 0)

    def fwd(v, d):
        return pltpu.roll(v, n_ext - d, 0)

    for gi, w in enumerate(POOL_WINDOWS):
        e = ext[:, gi * POOL_GROUP:(gi + 1) * POOL_GROUP]
        acc = e
        span = 1
        while span < w:
            acc = acc + fwd(acc, span)
            span *= 2
        half = w // 2
        centred = pltpu.roll(acc, half, 0)
        s = centred[POOL_HALO:POOL_HALO + TPOOL]
        cnt = (jnp.minimum(pos + (w - 1 - half), seq_len - 1) - jnp.maximum(pos - half, 0) + 1).astype(F32)
        pooled = (s / cnt - e[POOL_HALO:POOL_HALO + TPOOL]).astype(BF16)
        mixed = _dot(pooled, w_ref[gi])
        o_ref[:, gi * POOL_GROUP:(gi + 1) * POOL_GROUP] = (
            mixed * sc_ref[:, gi * POOL_GROUP:(gi + 1) * POOL_GROUP]).astype(BF16)


def _pool(tok, u, w_pool, pool_scale, e):
    n_tiles = tok.t // TPOOL
    per = TPOOL // POOL_HALO
    n_halo_blocks = tok.t // POOL_HALO
    return pl.pallas_call(
        functools.partial(_pool_kernel, n_prompt_tiles=tok.tp // TPOOL, lp=tok.lp, ls=tok.ls),
        out_shape=jax.ShapeDtypeStruct((tok.t, A_WIDTH), BF16),
        grid=(n_tiles,),
        in_specs=[
            pl.BlockSpec((TPOOL, A_WIDTH), lambda i: (i, 0)),
            pl.BlockSpec((POOL_HALO, A_WIDTH), lambda i: (jnp.maximum(i * per - 1, 0), 0)),
            pl.BlockSpec((POOL_HALO, A_WIDTH), lambda i: (jnp.minimum((i + 1) * per, n_halo_blocks - 1), 0)),
            pl.BlockSpec((None, len(POOL_WINDOWS), POOL_GROUP, POOL_GROUP), lambda i: (e, 0, 0, 0)),
            pl.BlockSpec((None, 1, A_WIDTH), lambda i: (e, 0, 0)),
        ],
        out_specs=pl.BlockSpec((TPOOL, A_WIDTH), lambda i: (i, 0)),
        compiler_params=_cparams("arbitrary"),
        name="pool_mix",
    )(u, u, u, w_pool, pool_scale.reshape(pool_scale.shape[0], 1, A_WIDTH))


def _normalised(pv, upper):
    swapped = pltpu.roll(pv, HEAD_DIM, 1)
    return swapped / pv if upper else pv / swapped


def _pair_lanes(lower_src, upper_src):
    lane = lax.broadcasted_iota(jnp.int32, lower_src.shape, 1)
    return jnp.where(lane < HEAD_DIM, lower_src, upper_src)


def _attention_pipeline(n_blocks, unroll, q_of, chunks_of, store, s_refs, mx_ref, pv_ref):
    def scores(n, pos, slot):
        q = q_of(n, pos)
        mx = None
        for c, (k_fn, _, bias_fn) in enumerate(chunks_of(n, pos)):
            s = _dot_nt(q, k_fn())
            if bias_fn is not None:
                s = s + bias_fn()
            s_refs[slot][:, c * KCHUNK:(c + 1) * KCHUNK] = s
            part = jnp.maximum(s[:, :LANES], s[:, LANES:])
            mx = part if mx is None else jnp.maximum(mx, part)
        mx_ref[slot] = mx

    def outputs(n, pos, slot):
        mx = mx_ref[slot]
        m = jnp.broadcast_to(mx.max(axis=-1, keepdims=True), mx.shape)
        pv = None
        for c, (_, v_fn, _) in enumerate(chunks_of(n, pos)):
            p = jnp.concatenate(
                [jnp.exp2(s_refs[slot][:, c * KCHUNK + h * LANES:c * KCHUNK + (h + 1) * LANES] - m)
                 for h in range(KCHUNK // LANES)], axis=-1).astype(BF16)
            t = _dot(p, v_fn())
            pv = t if pv is None else pv + t
        pv_ref[slot] = pv

    def finalize(n, pos, slot):
        store(n, pos, pv_ref[slot])

    _run_pipeline(n_blocks, unroll, scores, outputs, finalize, pv_ref)


def _run_pipeline(n_blocks, unroll, scores, outputs, finalize, pv_ref):
    assert n_blocks % unroll == 0 and unroll % 2 == 0
    if n_blocks == unroll:
        scores(0, 0, 0)
        for u in range(unroll):
            if u >= 2:
                finalize(u - 2, u - 2, u % 2)
            outputs(u, u, u % 2)
            if u + 1 < unroll:
                scores(u + 1, u + 1, 1 - u % 2)
        finalize(unroll - 2, unroll - 2, 0)
        finalize(unroll - 1, unroll - 1, 1)
        return

    pv_ref[...] = jnp.ones(pv_ref.shape, F32)
    scores(0, 0, 0)

    def body(j, carry):
        base = unroll * j
        for u in range(unroll):
            slot = u % 2
            pf = (u - 2) % unroll
            finalize(jnp.maximum(base + u - 2, pf), pf, slot)
            outputs(base + u, u, slot)
            ps = (u + 1) % unroll
            scores(jnp.minimum(base + u + 1, n_blocks - unroll + ps), ps, 1 - slot)
        return carry

    lax.fori_loop(0, n_blocks // unroll, body, 0)
    finalize(n_blocks - 2, unroll - 2, 0)
    finalize(n_blocks - 1, unroll - 1, 1)


def _pipeline_scratch(m_rows, n_keys):
    return [pltpu.VMEM((m_rows, n_keys), F32), pltpu.VMEM((m_rows, n_keys), F32),
            pltpu.VMEM((2, m_rows, LANES), F32), pltpu.VMEM((2, m_rows, LANES), F32)]


def _prompt_attn_kernel(q_ref, k_ref, v_ref, o_ref, s0_ref, s1_ref, mx_ref, pv_ref, *, group):
    n_q = q_ref.shape[0]
    seq = q_ref.shape[1]
    per = min(group, 2)
    n_blocks = n_q // per

    def q_of(n, pos):
        return jnp.concatenate([q_ref[per * pos + g] for g in range(per)], axis=0) if per > 1 else q_ref[pos]

    def chunks_of(n, pos):
        kv = per * pos // group
        return [(functools.partial(lambda c0: k_ref[kv, c0:c0 + KCHUNK, :], c0),
                 functools.partial(lambda c0: v_ref[kv, c0:c0 + KCHUNK, :], c0), None)
                for c0 in range(0, seq, KCHUNK)]

    def store(n, pos, pv):
        if per == 2:
            col = per * pos * HEAD_DIM
            o_ref[:, col:col + 2 * HEAD_DIM] = _pair_lanes(
                _normalised(pv, False)[:seq], _normalised(pv, True)[seq:]).astype(BF16)
        else:
            half = (pos % 2) * HEAD_DIM
            o_ref[:, pos * HEAD_DIM:(pos + 1) * HEAD_DIM] = (
                _normalised(pv, pos % 2 == 1)[:, half:half + HEAD_DIM].astype(BF16))

    _attention_pipeline(n_blocks, n_blocks, q_of, chunks_of, store, (s0_ref, s1_ref), mx_ref, pv_ref)


def _prompt_attn(tok, q, k, v, n_q_heads, n_kv_heads):
    group = n_q_heads // n_kv_heads
    assert tok.lp % KCHUNK == 0 and n_kv_heads % 2 == 0
    return pl.pallas_call(
        functools.partial(_prompt_attn_kernel, group=group),
        out_shape=jax.ShapeDtypeStruct((tok.tp, n_q_heads * HEAD_DIM), BF16),
        grid=(tok.bp,),
        in_specs=[
            pl.BlockSpec((n_q_heads, tok.lp, HEAD_DIM), lambda b: (0, b, 0)),
            pl.BlockSpec((n_kv_heads, tok.lp, HEAD_DIM), lambda b: (0, b, 0)),
            pl.BlockSpec((n_kv_heads, tok.lp, 2 * HEAD_DIM), lambda b: (0, b, 0)),
        ],
        out_specs=pl.BlockSpec((tok.lp, n_q_heads * HEAD_DIM), lambda b: (b, 0)),
        scratch_shapes=_pipeline_scratch(min(group, 2) * tok.lp, tok.lp),
        compiler_params=_cparams("arbitrary"),
        name="prompt_attn",
    )(q, k, v)


def _nb_attn_kernel(q_ref, k_ref, v_ref, ck_ref, cv_ref, rel_ref, rowmask_ref, o_ref,
                    s0_ref, s1_ref, mx_ref, pv_ref, *, rows, n_ctx):
    n_bands = rows // NB_BAND
    tq = NB_BAND * GRID_W

    def place(n, pos):
        return pos // 2, 2 * (n // PIPE_UNROLL) + pos % 2

    def q_of(n, pos):
        hh, band = place(n, pos)
        return q_ref[hh, pl.ds(pl.multiple_of(band * tq, tq), tq), :]

    def chunks_of(n, pos):
        hh, band = place(n, pos)
        ty = jnp.where(band == 0, 0, jnp.where(band == n_bands - 1, 2, 1))
        row0 = jnp.clip(NB_BAND * band - NB_ROWS // 2, 0, rows - NB_KROWS)
        start = row0 * GRID_W

        def bias(c0):
            strips = []
            for j in range(NB_BAND):
                off = (NB_BAND - 1 - j) * GRID_W
                copy = (off // GRID_W) % 2
                first = (off - copy * GRID_W + c0) // LANES
                strips.append(jnp.concatenate(
                    [rel_ref[ty, copy, first + t, hh] for t in range(KCHUNK // LANES)], axis=-1))
            return jnp.concatenate(strips, axis=0) + rowmask_ref[ty, :, c0:c0 + KCHUNK]

        def band_rows(ref, c0):
            return ref[hh, pl.ds(pl.multiple_of(start + c0, KCHUNK), KCHUNK), :]

        out = [(functools.partial(band_rows, k_ref, c0), functools.partial(band_rows, v_ref, c0),
                functools.partial(bias, c0)) for c0 in range(0, NB_KROWS * GRID_W, KCHUNK)]
        out += [(functools.partial(lambda c0: ck_ref[hh, c0:c0 + KCHUNK, :], c0),
                 functools.partial(lambda c0: cv_ref[hh, c0:c0 + KCHUNK, :], c0), None)
                for c0 in range(0, n_ctx, KCHUNK)]
        return out

    def store(n, pos, pv):
        hh, band = place(n, pos)
        half = hh * HEAD_DIM
        o_ref[pl.ds(pl.multiple_of(band * tq, tq), tq), half:half + HEAD_DIM] = (
            _normalised(pv, hh == 1)[:, half:half + HEAD_DIM].astype(BF16))

    _attention_pipeline(2 * n_bands, PIPE_UNROLL, q_of, chunks_of, store, (s0_ref, s1_ref), mx_ref, pv_ref)


def _nb_attn(tok, q, k, v, ck, cv, rel, rowmask, e):
    rows = tok.ls // GRID_W
    tq = NB_BAND * GRID_W
    kv0 = tok.tp // tok.ls
    past = ck.shape[2]
    assert tok.tp % tok.ls == 0 and past % KCHUNK == 0 and (rows // NB_BAND) % 2 == 0
    return pl.pallas_call(
        functools.partial(_nb_attn_kernel, rows=rows, n_ctx=past),
        out_shape=jax.ShapeDtypeStruct((tok.ts, NB_HEADS * HEAD_DIM), BF16),
        grid=(tok.bs, NB_HEADS // 2),
        in_specs=[
            pl.BlockSpec((2, tok.ls, HEAD_DIM), lambda b, p: (p, kv0 + b, 0)),
            pl.BlockSpec((2, tok.ls, HEAD_DIM), lambda b, p: (p, kv0 + b, 0)),
            pl.BlockSpec((2, tok.ls, 2 * HEAD_DIM), lambda b, p: (p, kv0 + b, 0)),
            pl.BlockSpec((None, 2, past, HEAD_DIM), lambda b, p: (b, p, 0, 0)),
            pl.BlockSpec((None, 2, past, 2 * HEAD_DIM), lambda b, p: (b, p, 0, 0)),
            pl.BlockSpec((None,) + rel.shape[1:4] + (2, GRID_W, LANES), lambda b, p: (e, 0, 0, 0, p, 0, 0)),
            _resident(rowmask.shape, lambda b, p: (0, 0, 0)),
        ],
        out_specs=pl.BlockSpec((tok.ls, 2 * HEAD_DIM), lambda b, p: (b, p)),
        scratch_shapes=_pipeline_scratch(tq, NB_KROWS * GRID_W + past),
        compiler_params=_cparams("arbitrary", "arbitrary"),
        name="nb_attn",
    )(q, k, v, ck, cv, rel, rowmask)


def _nb_band(band, rows):
    n_bands = rows // NB_BAND
    ty = 0 if band == 0 else (2 if band == n_bands - 1 else 1)
    row0 = int(np.clip(NB_BAND * band - NB_ROWS // 2, 0, rows - NB_KROWS))
    return ty, row0, row0 - NB_BAND * band + NB_ROWS - 1


def _nb_row_valid(rows):
    n_bands = rows // NB_BAND
    valid = np.zeros((3, NB_BAND, NB_KROWS), bool)
    for band in (0, 1, n_bands - 1):
        ty, row0, _ = _nb_band(band, rows)
        for jq in range(NB_BAND):
            r = NB_BAND * band + jq
            rs = int(np.clip(r - NB_ROWS // 2, 0, rows - NB_ROWS))
            for m in range(NB_KROWS):
                valid[ty, jq, m] = rs <= row0 + m < rs + NB_ROWS
    return valid


def _toeplitz_cols(rpb):
    w = GRID_W
    n_edge = w - NB_COLS
    lead = rpb.shape[:-1]
    e = jnp.concatenate([jnp.broadcast_to(rpb[..., :1], lead + (n_edge,)), rpb,
                         jnp.broadcast_to(rpb[..., -1:], lead + (n_edge + 1,))], axis=-1)
    flat = jnp.tile(e, (1,) * len(lead) + (w,))[..., :w * (2 * w - 1)]
    return flat.reshape(lead + (w, 2 * w - 1))[..., w - 1:]


def _nb_bias_tables(rpb, rows):
    n_layers, n_heads = rpb.shape[:2]
    n_bands = rows // NB_BAND
    cq = np.arange(GRID_W)
    cstart = np.clip(cq - NB_COLS // 2, 0, GRID_W - NB_COLS)
    col_valid = (cq[None, :] >= cstart[:, None]) & (cq[None, :] < cstart[:, None] + NB_COLS)
    blocks = jnp.where(col_valid, _toeplitz_cols(rpb) * LOG2E, NEG_INF)
    n_d = 2 * NB_ROWS - 1
    blocks = jnp.concatenate([blocks, jnp.zeros((n_layers, n_heads, 1, GRID_W, GRID_W), F32)], axis=2)
    offs = np.array([_nb_band(band, rows)[2] for band in (0, 1, n_bands - 1)])
    n_strip = NB_BAND + NB_KROWS
    d = (offs[:, None, None] - (NB_BAND - 1) + np.arange(2)[None, :, None]
         + np.arange(n_strip)[None, None, :])
    d = np.where((d >= 0) & (d < n_d), d, n_d)
    by_offset = jnp.transpose(blocks, (0, 2, 1, 3, 4))
    halves = [jnp.take(by_offset, jnp.asarray(d[:, :, half::2].reshape(-1)), axis=1) for half in range(2)]
    rel = jnp.concatenate(halves, axis=-1).reshape(n_layers, 3, 2, n_strip // 2, n_heads, GRID_W, 2 * GRID_W)
    small = jnp.where(jnp.asarray(_nb_row_valid(rows)), 0.0, NEG_INF).astype(F32)
    rowmask = jnp.broadcast_to(small[:, :, None, :, None], (3, NB_BAND, GRID_W, NB_KROWS, GRID_W))
    return rel, rowmask.reshape(3, NB_BAND * GRID_W, NB_KROWS * GRID_W)


def _gqa_attn_kernel(q_ref, k_ref, v_ref, ck_ref, cv_ref, o_ref, s0_ref, s1_ref, mx_ref, pv_ref,
                     *, group, n_ctx, n_self):
    def q_of(n, pos):
        r0 = pl.multiple_of(n * GQ_TOK, GQ_TOK)
        return jnp.concatenate([q_ref[g, pl.ds(r0, GQ_TOK), :] for g in range(group)], axis=0)

    def chunks_of(n, pos):
        out = [(functools.partial(lambda c0: ck_ref[0, c0:c0 + KCHUNK, :], c0),
                functools.partial(lambda c0: cv_ref[0, c0:c0 + KCHUNK, :], c0), None)
               for c0 in range(0, n_ctx, KCHUNK)]
        out += [(functools.partial(lambda c0: k_ref[0, c0:c0 + KCHUNK, :], c0),
                 functools.partial(lambda c0: v_ref[0, c0:c0 + KCHUNK, :], c0), None)
                for c0 in range(0, n_self, KCHUNK)]
        return out

    def store(n, pos, pv):
        r0 = pl.multiple_of(n * GQ_TOK, GQ_TOK)
        lower, upper = _normalised(pv, False), _normalised(pv, True)
        pairs = [_pair_lanes(lower[g * GQ_TOK:(g + 1) * GQ_TOK], upper[(g + 1) * GQ_TOK:(g + 2) * GQ_TOK])
                 for g in range(0, group, 2)]
        o_ref[pl.ds(r0, GQ_TOK), :] = jnp.concatenate(pairs, axis=-1).astype(BF16)

    _attention_pipeline(q_ref.shape[1] // GQ_TOK, PIPE_UNROLL, q_of, chunks_of, store,
                        (s0_ref, s1_ref), mx_ref, pv_ref)


def _gqa_attn(tok, q, k, v, ck, cv):
    group = C_Q_HEADS // C_KV_HEADS
    kv0 = tok.tp // tok.ls
    past = ck.shape[2]
    assert past % KCHUNK == 0 and tok.ls % KCHUNK == 0 and tok.ls % (GQ_TOK * PIPE_UNROLL) == 0 and group % 2 == 0
    return pl.pallas_call(
        functools.partial(_gqa_attn_kernel, group=group, n_ctx=past, n_self=tok.ls),
        out_shape=jax.ShapeDtypeStruct((tok.ts, C_Q_HEADS * HEAD_DIM), BF16),
        grid=(tok.bs, C_KV_HEADS),
        in_specs=[
            pl.BlockSpec((group, tok.ls, HEAD_DIM), lambda b, g: (g, kv0 + b, 0)),
            pl.BlockSpec((1, tok.ls, HEAD_DIM), lambda b, g: (g, kv0 + b, 0)),
            pl.BlockSpec((1, tok.ls, 2 * HEAD_DIM), lambda b, g: (g, kv0 + b, 0)),
            pl.BlockSpec((None, 1, past, HEAD_DIM), lambda b, g: (b, g, 0, 0)),
            pl.BlockSpec((None, 1, past, 2 * HEAD_DIM), lambda b, g: (b, g, 0, 0)),
        ],
        out_specs=pl.BlockSpec((tok.ls, group * HEAD_DIM), lambda b, g: (b, g)),
        scratch_shapes=_pipeline_scratch(group * GQ_TOK, past + tok.ls),
        compiler_params=_cparams("arbitrary", "arbitrary"),
        name="gqa_attn",
    )(q, k, v, ck, cv)


def _partner(x, s, lane):
    return jnp.where((lane & s) != 0, pltpu.roll(x, s, 1), pltpu.roll(x, LANES - s, 1))


def _head_rms(x, g, seg_ones):
    y = x * x
    hi = y.astype(BF16)
    lo = (y - hi.astype(F32)).astype(BF16)
    ss = _dot(hi, seg_ones) + _dot(lo, seg_ones)
    return x * lax.rsqrt(ss * (1.0 / HEAD_DIM) + EPS) * g


def _rope(x, cos, sin, lane):
    return x * cos + _partner(x, HEAD_DIM // 4, lane) * sin


def _proj_c_kernel(x_ref, mod_ref, g_ref, w_ref, gq_ref, gk_ref, cos_ref, sin_ref,
                   q_ref, k_ref, v_ref, kf_ref, vf_ref, *, npt):
    i = pl.program_id(0)
    h = _norm_mod(x_ref[...], g_ref[...], mod_ref[3:4, :], mod_ref[4:5, :]).astype(BF16)
    proj = _dot(h, w_ref[...])
    nq = C_Q_HEADS * HEAD_DIM
    nk = C_KV_HEADS * HEAD_DIM
    lane = lax.broadcasted_iota(jnp.int32, (TM, LANES), 1)
    seg_ones = (lax.broadcasted_iota(jnp.int32, (LANES, LANES), 0) // HEAD_DIM ==
                lax.broadcasted_iota(jnp.int32, (LANES, LANES), 1) // HEAD_DIM).astype(BF16)
    cos = cos_ref[...]
    sin = sin_ref[...]
    gq = gq_ref[...]
    gk = gk_ref[...]
    for c in range(nq // LANES):
        y = _head_rms(proj[:, c * LANES:(c + 1) * LANES], gq, seg_ones)
        y = _rope(y, cos, sin, lane) * Q_SCALE
        q_ref[2 * c] = y[:, :HEAD_DIM].astype(BF16)
        q_ref[2 * c + 1] = y[:, HEAD_DIM:].astype(BF16)
    for c in range(nk // LANES):
        yk = _head_rms(proj[:, nq + c * LANES:nq + (c + 1) * LANES], gk, seg_ones)

        @pl.when(i < npt)
        def _(yk=yk, c=c):
            kf_ref[:, c * LANES:(c + 1) * LANES] = yk

        yr = _rope(yk, cos, sin, lane)
        k_ref[2 * c] = yr[:, :HEAD_DIM].astype(BF16)
        k_ref[2 * c + 1] = yr[:, HEAD_DIM:].astype(BF16)
    for hh in range(C_KV_HEADS):
        v_ref[hh] = _with_ones(proj[:, nq + nk + hh * HEAD_DIM:nq + nk + (hh + 1) * HEAD_DIM])

    @pl.when(i < npt)
    def _():
        vf_ref[...] = proj[:, nq + nk:nq + 2 * nk]


def _rope_tables(tok):
    n = HEAD_DIM // 4
    inv = ROPE_THETA ** (-jnp.arange(n, dtype=F32) / n)
    t = jnp.arange(tok.ls)
    pos = jnp.stack([(t // GRID_W).astype(F32), (t % GRID_W).astype(F32)], axis=1)
    ang = pos[:, :, None] * inv[None, None, :]
    cos = jnp.cos(ang)
    sin = jnp.sin(ang)
    cos_h = jnp.concatenate([cos, cos], axis=-1).reshape(tok.ls, HEAD_DIM)
    sin_h = jnp.concatenate([-sin, sin], axis=-1).reshape(tok.ls, HEAD_DIM)
    cos_t = jnp.concatenate([jnp.ones((TM, HEAD_DIM), F32), cos_h], axis=0)
    sin_t = jnp.concatenate([jnp.zeros((TM, HEAD_DIM), F32), sin_h], axis=0)
    return jnp.tile(cos_t, (1, 2)), jnp.tile(sin_t, (1, 2))


def _proj_c(tok, x, mod, g_norm, w, g_q, g_k, cos_t, sin_t, layer):
    o = layer // 2
    nk = C_KV_HEADS * HEAD_DIM

    def rope_idx(i):
        return (jnp.where(i < tok.npt, 0, 1 + (i - tok.npt) % tok.tiles_per_seq), 0)

    (qs, qsh), (ks, ksh), (vs, vsh) = (_head_major(C_Q_HEADS, tok), _head_major(C_KV_HEADS, tok),
                                       _head_major(C_KV_HEADS, tok, 2 * HEAD_DIM))
    vec = pl.BlockSpec((None, 1, LANES), lambda i: (o, 0, 0))
    n = g_q.shape[0]
    return pl.pallas_call(
        functools.partial(_proj_c_kernel, npt=tok.npt),
        out_shape=(qsh, ksh, vsh, jax.ShapeDtypeStruct((tok.tp, nk), F32), jax.ShapeDtypeStruct((tok.tp, nk), F32)),
        grid=(tok.nt,),
        in_specs=[tok.x_spec(), tok.mod_spec(layer), _vec_spec(layer, 1),
                  _resident((None,) + w.shape[1:], lambda i: (o, 0, 0)), vec, vec,
                  pl.BlockSpec((TM, LANES), rope_idx), pl.BlockSpec((TM, LANES), rope_idx)],
        out_specs=(qs, ks, vs, tok.prompt_spec(nk), tok.prompt_spec(nk)),
        compiler_params=_cparams("arbitrary"),
        name="proj_c",
    )(x, mod, g_norm, w, jnp.tile(g_q, (1, 2)).reshape(n, 1, LANES), jnp.tile(g_k, (1, 2)).reshape(n, 1, LANES),
      cos_t, sin_t)


def _ctx_heads(cache, with_ones=False):
    t = jnp.transpose(cache, (0, 2, 1, 3))
    if with_ones:
        t = jnp.concatenate([t, jnp.ones_like(t)], axis=-1)
    return t.astype(BF16)


def kernel(x_prompt, x_sample, cache_nb_k, cache_nb_v, cache_attn_k, cache_attn_v, c, c_ctx, w_mod, b_mod, g_norm, w_ffn_in, w_ffn_out, w_in_ab, w_pool, pool_scale, nb_rpb, w_out_ab, w_qkv_c, g_qnorm, g_knorm, w_out_c, g_final):
    bp, lp, d = x_prompt.shape
    bs, ls, _ = x_sample.shape
    assert d == D_MODEL and w_ffn_in.shape[-1] == 2 * D_FF and w_mod.shape[0] == DEPTH
    tok = _Tokens(bp, lp, bs, ls)

    n_rows = SUBLANES * (-(-(1 + bs) // SUBLANES))
    cvec = jnp.concatenate([c_ctx[None], c, jnp.zeros((n_rows - 1 - bs, d), F32)], axis=0)
    mod = _ada(cvec, w_mod, b_mod).reshape(DEPTH, n_rows, N_MOD, d)
    g_norm4 = g_norm.reshape(DEPTH, 3, 1, d)

    w_in_ab_b = w_in_ab.astype(BF16)
    w_pool_b = w_pool.astype(BF16)
    w_out_ab_b = w_out_ab.astype(BF16)
    w_qkv_c_b = w_qkv_c.astype(BF16)
    w_out_c_b = w_out_c.astype(BF16)
    cos_t, sin_t = _rope_tables(tok)
    rel, rowmask = _nb_bias_tables(nb_rpb, ls // GRID_W)

    x = (x_prompt.reshape(tok.tp, d), x_sample.reshape(tok.ts, d))
    nb_k, nb_v, at_k, at_v = [], [], [], []
    for l in range(DEPTH):
        x = _ffn(tok, x, mod, g_norm4, w_ffn_in, w_ffn_out, l, 0)
        if l % 2 == 0:
            e = l // 2
            u, q, k, v, kf, vf = _proj_ab(tok, x, mod, g_norm4, w_in_ab_b, l)
            a = _pool(tok, u, w_pool_b, pool_scale, e)
            b_p = _prompt_attn(tok, q, k, v, NB_HEADS, NB_HEADS)
            b_s = _nb_attn(tok, q, k, v, _ctx_heads(cache_nb_k[:, e]), _ctx_heads(cache_nb_v[:, e], True),
                           rel, rowmask, e)
            mixer = (a, b_p, b_s, w_out_ab_b)
            nb_k.append(kf.reshape(bp, lp, NB_HEADS, HEAD_DIM))
            nb_v.append(vf.reshape(bp, lp, NB_HEADS, HEAD_DIM))
        else:
            o = l // 2
            q, k, v, kf, vf = _proj_c(tok, x, mod, g_norm4, w_qkv_c_b, g_qnorm, g_knorm, cos_t, sin_t, l)
            b_p = _prompt_attn(tok, q, k, v, C_Q_HEADS, C_KV_HEADS)
            b_s = _gqa_attn(tok, q, k, v, _ctx_heads(cache_attn_k[:, o]), _ctx_heads(cache_attn_v[:, o], True))
            mixer = (None, b_p, b_s, w_out_c_b)
            at_k.append(kf.reshape(bp, lp, C_KV_HEADS, HEAD_DIM))
            at_v.append(vf.reshape(bp, lp, C_KV_HEADS, HEAD_DIM))
        x = _ffn(tok, x, mod, g_norm4, w_ffn_in, w_ffn_out, l, 2, g_final=g_final if l == DEPTH - 1 else None,
                 mixer=mixer)

    y_prompt, y_sample = x
    return (y_prompt.reshape(bp, lp, d), y_sample.reshape(bs, ls, d), jnp.stack(nb_k, axis=1),
            jnp.stack(nb_v, axis=1), jnp.stack(at_k, axis=1), jnp.stack(at_v, axis=1))
```

```python
import functools
import math

import numpy as np
import jax
import jax.numpy as jnp
from jax import lax
from jax.experimental import pallas as pl
from jax.experimental.pallas import tpu as pltpu

F32 = jnp.float32
BF16 = jnp.bfloat16

D_MODEL = 1024
DEPTH = 4
GRID_W = 64
D_FF = 2816
N_MOD = 9
A_WIDTH = 512
POOL_WINDOWS = (2, 4, 8, 16)
POOL_GROUP = 128
HEAD_DIM = 64
NB_HEADS = 8
NB_ROWS = 8
NB_COLS = 16
C_Q_HEADS = 16
C_KV_HEADS = 4
ROPE_THETA = 10000.0
EPS = 1e-6
NEG_INF = -1e30
LOG2E = math.log2(math.e)
Q_SCALE = HEAD_DIM ** -0.5 * LOG2E

LANES = 128
SUBLANES = 8
VMEM_LIMIT_BYTES = 56 * 1024 * 1024

ADA_TN = 2304
TM = 512
TF = 256
TPOOL = 256
POOL_TILES = 2
POOL_HALO = 8
NB_BAND = 8
NB_KROWS = 16
GQ_TOK = 128
PIPE_UNROLL = 4
KCHUNK = 256


def _cparams(*sem):
    return pltpu.CompilerParams(dimension_semantics=sem, vmem_limit_bytes=VMEM_LIMIT_BYTES)


def _resident(shape, index_map):
    return pl.BlockSpec(shape, index_map, pipeline_mode=pl.Buffered(1))


def _dot(a, b):
    return jnp.dot(a, b, preferred_element_type=F32)


def _dot_nt(a, b):
    return lax.dot_general(a, b, (((1,), (1,)), ((), ())), preferred_element_type=F32)


def _norm_mod(x, g, shift, scale):
    ms = jnp.mean(x * x, axis=-1, keepdims=True)
    y = x * lax.rsqrt(ms + EPS) * g
    return y * (1.0 + scale) + shift


def _with_ones(v):
    return jnp.concatenate([v, jnp.ones_like(v)], axis=-1).astype(BF16)


def _ada_kernel(c_ref, w_ref, b_ref, o_ref):
    c = c_ref[...]
    s = (c * jax.nn.sigmoid(c)).astype(BF16)
    o_ref[...] = _dot(s, w_ref[...].astype(BF16)) + b_ref[...]


def _ada(cvec, w_mod, b_mod):
    rows = cvec.shape[0]
    n = w_mod.shape[-1]
    tn = ADA_TN
    assert n % tn == 0
    return pl.pallas_call(
        _ada_kernel,
        out_shape=jax.ShapeDtypeStruct((DEPTH, rows, n), F32),
        grid=(DEPTH, n // tn),
        in_specs=[
            pl.BlockSpec((rows, D_MODEL), lambda l, j: (0, 0)),
            pl.BlockSpec((None, D_MODEL, tn), lambda l, j: (l, 0, j)),
            pl.BlockSpec((None, 1, tn), lambda l, j: (l, 0, j)),
        ],
        out_specs=pl.BlockSpec((None, rows, tn), lambda l, j: (l, 0, j)),
        compiler_params=_cparams("arbitrary", "arbitrary"),
        name="ada_mod",
    )(cvec, w_mod, b_mod.reshape(DEPTH, 1, n))


class _Tokens:
    def __init__(self, bp, lp, bs, ls):
        self.bp, self.lp, self.bs, self.ls = bp, lp, bs, ls
        self.tp = bp * lp
        self.ts = bs * ls
        self.t = self.tp + self.ts
        assert self.tp % TM == 0 and ls % TM == 0 and lp % TPOOL == 0 and ls % TPOOL == 0
        self.npt = self.tp // TM
        self.nt = self.t // TM
        self.tiles_per_seq = ls // TM

    def mod_row(self, i):
        return jnp.where(i < self.npt, 0, 1 + (i - self.npt) // self.tiles_per_seq)

    def x_spec(self, width=D_MODEL):
        return pl.BlockSpec((TM, width), lambda i: (i, 0))

    def mod_spec(self, layer):
        return pl.BlockSpec((None, None, N_MOD, D_MODEL), lambda i: (layer, self.mod_row(i), 0, 0))

    def prompt_spec(self, width):
        return pl.BlockSpec((TM, width), lambda i: (jnp.minimum(i, self.npt - 1), 0))

    def sample_spec(self, width):
        return pl.BlockSpec((TM, width), lambda i: (jnp.maximum(i - self.npt, 0), 0))


def _vec_spec(layer, sub):
    return pl.BlockSpec((None, None, 1, D_MODEL), lambda i: (layer, sub, 0, 0))


def _ffn_kernel(*refs, sub, npt, split_in, mix, final):
    refs = list(refs)
    i = pl.program_id(0)
    if split_in:
        xp_ref, xs_ref = refs[:2]
        refs = refs[2:]
        x = jnp.where(i < npt, xp_ref[...], xs_ref[...])
    else:
        x = refs.pop(0)[...]
    if mix is not None:
        a_ref = refs.pop(0) if mix == "pool+attn" else None
        bp_ref, bs_ref, wmix_ref = refs[:3]
        refs = refs[3:]
    mod_ref, g_ref, win_ref, wout_ref = refs[:4]
    refs = refs[4:]
    if final:
        gf_ref, yp_ref, ys_ref = refs
    else:
        (o_ref,) = refs
    if mix is not None:
        b = jnp.where(i < npt, bp_ref[...], bs_ref[...])
        if a_ref is None:
            mixed = _dot(b, wmix_ref[...])
        else:
            mixed = _dot(a_ref[...], wmix_ref[:A_WIDTH, :]) + _dot(b, wmix_ref[A_WIDTH:, :])
        x = x + mod_ref[5:6, :] * mixed
    h = _norm_mod(x, g_ref[...], mod_ref[3 * sub:3 * sub + 1, :], mod_ref[3 * sub + 1:3 * sub + 2, :]).astype(BF16)
    acc = None
    for j in range(D_FF // TF):
        a = _dot(h, win_ref[:, j * TF:(j + 1) * TF].astype(BF16))
        b = _dot(h, win_ref[:, D_FF + j * TF:D_FF + (j + 1) * TF].astype(BF16))
        act = (a * jax.nn.sigmoid(a) * b).astype(BF16)
        t = _dot(act, wout_ref[j * TF:(j + 1) * TF, :].astype(BF16))
        acc = t if acc is None else acc + t
    y = x + mod_ref[3 * sub + 2:3 * sub + 3, :] * (0.5 * acc)
    if final:
        ms = jnp.mean(y * y, axis=-1, keepdims=True)
        y = y * lax.rsqrt(ms + EPS) * gf_ref[...]

        @pl.when(i < npt)
        def _():
            yp_ref[...] = y

        @pl.when(i >= npt)
        def _():
            ys_ref[...] = y
    else:
        o_ref[...] = y


def _ffn(tok, x, mod, g_norm, w_in, w_out, layer, sub, g_final=None, mixer=None):
    split_in = isinstance(x, tuple)
    final = g_final is not None
    which = sub // 2
    mix = None
    if split_in:
        in_specs = [tok.prompt_spec(D_MODEL), tok.sample_spec(D_MODEL)]
        args = list(x)
    else:
        in_specs = [tok.x_spec()]
        args = [x]
    if mixer is not None:
        a, b_prompt, b_sample, w_mix = mixer
        mix = "attn" if a is None else "pool+attn"
        bw = b_prompt.shape[1]
        if a is not None:
            in_specs.append(tok.x_spec(A_WIDTH))
            args.append(a)
        in_specs += [tok.prompt_spec(bw), tok.sample_spec(bw),
                     _resident((None,) + w_mix.shape[1:], lambda i: (layer // 2, 0, 0))]
        args += [b_prompt, b_sample, w_mix]
    in_specs += [
        tok.mod_spec(layer), _vec_spec(layer, sub),
        _resident((None, None, D_MODEL, 2 * D_FF), lambda i: (layer, which, 0, 0)),
        _resident((None, None, D_FF, D_MODEL), lambda i: (layer, which, 0, 0)),
    ]
    args += [mod, g_norm, w_in, w_out]
    if final:
        in_specs.append(pl.BlockSpec((1, D_MODEL), lambda i: (0, 0)))
        args.append(g_final.reshape(1, D_MODEL))
        out_shape = (jax.ShapeDtypeStruct((tok.tp, D_MODEL), F32), jax.ShapeDtypeStruct((tok.ts, D_MODEL), F32))
        out_specs = (tok.prompt_spec(D_MODEL), tok.sample_spec(D_MODEL))
    else:
        out_shape = jax.ShapeDtypeStruct((tok.t, D_MODEL), F32)
        out_specs = tok.x_spec()
    return pl.pallas_call(
        functools.partial(_ffn_kernel, sub=sub, npt=tok.npt, split_in=split_in, mix=mix, final=final),
        out_shape=out_shape,
        grid=(tok.nt,),
        in_specs=in_specs,
        out_specs=out_specs,
        compiler_params=_cparams("arbitrary"),
        name="ffn_final" if final else ("ffn_first" if split_in else ("ffn" if mix is None else "mix_ffn")),
    )(*args)


def _proj_ab_kernel(x_ref, mod_ref, g_ref, w_ref, u_ref, q_ref, k_ref, v_ref, kf_ref, vf_ref, *, npt):
    i = pl.program_id(0)
    h = _norm_mod(x_ref[...], g_ref[...], mod_ref[3:4, :], mod_ref[4:5, :]).astype(BF16)
    proj = _dot(h, w_ref[...])
    u_ref[...] = proj[:, :A_WIDTH]
    hw = NB_HEADS * HEAD_DIM
    for hh in range(NB_HEADS):
        lo = A_WIDTH + hh * HEAD_DIM
        q_ref[hh] = (proj[:, lo:lo + HEAD_DIM] * Q_SCALE).astype(BF16)
        k_ref[hh] = proj[:, lo + hw:lo + hw + HEAD_DIM].astype(BF16)
        v_ref[hh] = _with_ones(proj[:, lo + 2 * hw:lo + 2 * hw + HEAD_DIM])

    @pl.when(i < npt)
    def _():
        kf_ref[...] = proj[:, A_WIDTH + hw:A_WIDTH + 2 * hw]
        vf_ref[...] = proj[:, A_WIDTH + 2 * hw:A_WIDTH + 3 * hw]


def _head_major(n_heads, tok, width=HEAD_DIM):
    return (pl.BlockSpec((n_heads, TM, width), lambda i: (0, i, 0)),
            jax.ShapeDtypeStruct((n_heads, tok.t, width), BF16))


def _proj_ab(tok, x, mod, g_norm, w, layer):
    e = layer // 2
    hw = NB_HEADS * HEAD_DIM
    (qs, qsh), (ks, ksh), (vs, vsh) = (_head_major(NB_HEADS, tok), _head_major(NB_HEADS, tok),
                                       _head_major(NB_HEADS, tok, 2 * HEAD_DIM))
    return pl.pallas_call(
        functools.partial(_proj_ab_kernel, npt=tok.npt),
        out_shape=(jax.ShapeDtypeStruct((tok.t, A_WIDTH), F32), qsh, ksh, vsh,
                   jax.ShapeDtypeStruct((tok.tp, hw), F32), jax.ShapeDtypeStruct((tok.tp, hw), F32)),
        grid=(tok.nt,),
        in_specs=[tok.x_spec(), tok.mod_spec(layer), _vec_spec(layer, 1),
                  _resident((None,) + w.shape[1:], lambda i: (e, 0, 0))],
        out_specs=(tok.x_spec(A_WIDTH), qs, ks, vs, tok.prompt_spec(hw), tok.prompt_spec(hw)),
        compiler_params=_cparams("arbitrary"),
        name="proj_ab",
    )(x, mod, g_norm, w)


def _pool_kernel(u_ref, prev_ref, next_ref, w_ref, sc_ref, o_ref, *, n_prompt_tiles, lp, ls):
    i = pl.program_id(0)
    is_p = i < n_prompt_tiles
    steps_s = ls // (POOL_TILES * TPOOL)
    j = jnp.where(is_p, 0, (i - n_prompt_tiles) % steps_s)
    seq_len = jnp.where(is_p, lp, ls)
    n_ext = TPOOL + 2 * POOL_HALO

    def fwd(v, d):
        return pltpu.roll(v, n_ext - d, 0)

    for t in range(POOL_TILES):
        r0 = t * TPOOL
        pos0 = jnp.where(is_p, 0, (j * POOL_TILES + t) * TPOOL)
        first = pos0 == 0
        last = pos0 + TPOOL == seq_len
        before = prev_ref[...] if t == 0 else u_ref[r0 - POOL_HALO:r0, :]
        after = next_ref[...] if t == POOL_TILES - 1 else u_ref[r0 + TPOOL:r0 + TPOOL + POOL_HALO, :]
        prev = jnp.where(first, 0.0, before)
        nxt = jnp.where(last, 0.0, after)
        ext = jnp.concatenate([prev, u_ref[r0:r0 + TPOOL, :], nxt], axis=0)
        pos = pos0 + lax.broadcasted_iota(jnp.int32, (TPOOL, 1), 0)
        for gi, w in enumerate(POOL_WINDOWS):
            e = ext[:, gi * POOL_GROUP:(gi + 1) * POOL_GROUP]
            acc = e
            span = 1
            while span < w:
                acc = acc + fwd(acc, span)
                span *= 2
            half = w // 2
            centred = pltpu.roll(acc, half, 0)
            s = centred[POOL_HALO:POOL_HALO + TPOOL]
            cnt = (jnp.minimum(pos + (w - 1 - half), seq_len - 1) - jnp.maximum(pos - half, 0) + 1).astype(F32)
            pooled = (s / cnt - e[POOL_HALO:POOL_HALO + TPOOL]).astype(BF16)
            mixed = _dot(pooled, w_ref[gi])
            o_ref[r0:r0 + TPOOL, gi * POOL_GROUP:(gi + 1) * POOL_GROUP] = (
                mixed * sc_ref[:, gi * POOL_GROUP:(gi + 1) * POOL_GROUP]).astype(BF16)


def _pool(tok, u, w_pool, pool_scale, e):
    step = POOL_TILES * TPOOL
    assert tok.lp == TPOOL and tok.tp % step == 0 and tok.ls % step == 0
    n_tiles = tok.t // step
    per = step // POOL_HALO
    n_halo_blocks = tok.t // POOL_HALO
    return pl.pallas_call(
        functools.partial(_pool_kernel, n_prompt_tiles=tok.tp // step, lp=tok.lp, ls=tok.ls),
        out_shape=jax.ShapeDtypeStruct((tok.t, A_WIDTH), BF16),
        grid=(n_tiles,),
        in_specs=[
            pl.BlockSpec((step, A_WIDTH), lambda i: (i, 0)),
            pl.BlockSpec((POOL_HALO, A_WIDTH), lambda i: (jnp.maximum(i * per - 1, 0), 0)),
            pl.BlockSpec((POOL_HALO, A_WIDTH), lambda i: (jnp.minimum((i + 1) * per, n_halo_blocks - 1), 0)),
            pl.BlockSpec((None, len(POOL_WINDOWS), POOL_GROUP, POOL_GROUP), lambda i: (e, 0, 0, 0)),
            pl.BlockSpec((None, 1, A_WIDTH), lambda i: (e, 0, 0)),
        ],
        out_specs=pl.BlockSpec((step, A_WIDTH), lambda i: (i, 0)),
        compiler_params=_cparams("arbitrary"),
        name="pool_mix",
    )(u, u, u, w_pool, pool_scale.reshape(pool_scale.shape[0], 1, A_WIDTH))


def _normalised(pv, upper):
    swapped = pltpu.roll(pv, HEAD_DIM, 1)
    return swapped / pv if upper else pv / swapped


def _pair_lanes(lower_src, upper_src):
    lane = lax.broadcasted_iota(jnp.int32, lower_src.shape, 1)
    return jnp.where(lane < HEAD_DIM, lower_src, upper_src)


def _attention_pipeline(n_blocks, unroll, q_of, chunks_of, store, s_refs, mx_ref, pv_ref):
    def scores(n, pos, slot):
        q = q_of(n, pos)
        mx = None
        for c, (k_fn, _, bias_fn) in enumerate(chunks_of(n, pos)):
            s = _dot_nt(q, k_fn())
            if bias_fn is not None:
                s = s + bias_fn()
            s_refs[slot][:, c * KCHUNK:(c + 1) * KCHUNK] = s
            part = jnp.maximum(s[:, :LANES], s[:, LANES:])
            mx = part if mx is None else jnp.maximum(mx, part)
        mx_ref[slot] = mx

    def outputs(n, pos, slot):
        mx = mx_ref[slot]
        m = jnp.broadcast_to(mx.max(axis=-1, keepdims=True), mx.shape)
        pv = None
        for c, (_, v_fn, _) in enumerate(chunks_of(n, pos)):
            p = jnp.concatenate(
                [jnp.exp2(s_refs[slot][:, c * KCHUNK + h * LANES:c * KCHUNK + (h + 1) * LANES] - m)
                 for h in range(KCHUNK // LANES)], axis=-1).astype(BF16)
            t = _dot(p, v_fn())
            pv = t if pv is None else pv + t
        pv_ref[slot] = pv

    def finalize(n, pos, slot):
        store(n, pos, pv_ref[slot])

    _run_pipeline(n_blocks, unroll, scores, outputs, finalize, pv_ref)


def _run_pipeline(n_blocks, unroll, scores, outputs, finalize, pv_ref):
    assert n_blocks % unroll == 0 and unroll % 2 == 0
    if n_blocks == unroll:
        scores(0, 0, 0)
        for u in range(unroll):
            if u >= 2:
                finalize(u - 2, u - 2, u % 2)
            outputs(u, u, u % 2)
            if u + 1 < unroll:
                scores(u + 1, u + 1, 1 - u % 2)
        finalize(unroll - 2, unroll - 2, 0)
        finalize(unroll - 1, unroll - 1, 1)
        return

    pv_ref[...] = jnp.ones(pv_ref.shape, F32)
    scores(0, 0, 0)

    def body(j, carry):
        base = unroll * j
        for u in range(unroll):
            slot = u % 2
            pf = (u - 2) % unroll
            finalize(jnp.maximum(base + u - 2, pf), pf, slot)
            outputs(base + u, u, slot)
            ps = (u + 1) % unroll
            scores(jnp.minimum(base + u + 1, n_blocks - unroll + ps), ps, 1 - slot)
        return carry

    lax.fori_loop(0, n_blocks // unroll, body, 0)
    finalize(n_blocks - 2, unroll - 2, 0)
    finalize(n_blocks - 1, unroll - 1, 1)


def _pipeline_scratch(m_rows, n_keys):
    return [pltpu.VMEM((m_rows, n_keys), F32), pltpu.VMEM((m_rows, n_keys), F32),
            pltpu.VMEM((2, m_rows, LANES), F32), pltpu.VMEM((2, m_rows, LANES), F32)]


def _prompt_attn_kernel(q_ref, k_ref, v_ref, o_ref, s0_ref, s1_ref, mx_ref, pv_ref, *, group):
    n_q = q_ref.shape[0]
    seq = q_ref.shape[1]
    per = min(group, 2)
    n_blocks = n_q // per

    def q_of(n, pos):
        return jnp.concatenate([q_ref[per * pos + g] for g in range(per)], axis=0) if per > 1 else q_ref[pos]

    def chunks_of(n, pos):
        kv = per * pos // group
        return [(functools.partial(lambda c0: k_ref[kv, c0:c0 + KCHUNK, :], c0),
                 functools.partial(lambda c0: v_ref[kv, c0:c0 + KCHUNK, :], c0), None)
                for c0 in range(0, seq, KCHUNK)]

    def store(n, pos, pv):
        if per == 2:
            col = per * pos * HEAD_DIM
            o_ref[:, col:col + 2 * HEAD_DIM] = _pair_lanes(
                _normalised(pv, False)[:seq], _normalised(pv, True)[seq:]).astype(BF16)
        else:
            half = (pos % 2) * HEAD_DIM
            o_ref[:, pos * HEAD_DIM:(pos + 1) * HEAD_DIM] = (
                _normalised(pv, pos % 2 == 1)[:, half:half + HEAD_DIM].astype(BF16))

    _attention_pipeline(n_blocks, n_blocks, q_of, chunks_of, store, (s0_ref, s1_ref), mx_ref, pv_ref)


def _prompt_attn(tok, q, k, v, n_q_heads, n_kv_heads):
    group = n_q_heads // n_kv_heads
    assert tok.lp % KCHUNK == 0 and n_kv_heads % 2 == 0
    return pl.pallas_call(
        functools.partial(_prompt_attn_kernel, group=group),
        out_shape=jax.ShapeDtypeStruct((tok.tp, n_q_heads * HEAD_DIM), BF16),
        grid=(tok.bp,),
        in_specs=[
            pl.BlockSpec((n_q_heads, tok.lp, HEAD_DIM), lambda b: (0, b, 0)),
            pl.BlockSpec((n_kv_heads, tok.lp, HEAD_DIM), lambda b: (0, b, 0)),
            pl.BlockSpec((n_kv_heads, tok.lp, 2 * HEAD_DIM), lambda b: (0, b, 0)),
        ],
        out_specs=pl.BlockSpec((tok.lp, n_q_heads * HEAD_DIM), lambda b: (b, 0)),
        scratch_shapes=_pipeline_scratch(min(group, 2) * tok.lp, tok.lp),
        compiler_params=_cparams("arbitrary"),
        name="prompt_attn",
    )(q, k, v)


def _nb_attn_kernel(q_ref, k_ref, v_ref, ck_ref, cv_ref, rel_ref, rowmask_ref, o_ref,
                    s0_ref, s1_ref, mx_ref, pv_ref, *, rows, n_ctx):
    n_bands = rows // NB_BAND
    tq = NB_BAND * GRID_W

    def place(n, pos):
        return pos // 2, 2 * (n // PIPE_UNROLL) + pos % 2

    def q_of(n, pos):
        hh, band = place(n, pos)
        return q_ref[hh, pl.ds(pl.multiple_of(band * tq, tq), tq), :]

    def chunks_of(n, pos):
        hh, band = place(n, pos)
        ty = jnp.where(band == 0, 0, jnp.where(band == n_bands - 1, 2, 1))
        row0 = jnp.clip(NB_BAND * band - NB_ROWS // 2, 0, rows - NB_KROWS)
        start = row0 * GRID_W

        def bias(c0):
            strips = []
            for j in range(NB_BAND):
                off = (NB_BAND - 1 - j) * GRID_W
                copy = (off // GRID_W) % 2
                first = (off - copy * GRID_W + c0) // LANES
                strips.append(jnp.concatenate(
                    [rel_ref[ty, copy, first + t, hh] for t in range(KCHUNK // LANES)], axis=-1))
            return jnp.concatenate(strips, axis=0) + rowmask_ref[ty, :, c0:c0 + KCHUNK]

        def band_rows(ref, c0):
            return ref[hh, pl.ds(pl.multiple_of(start + c0, KCHUNK), KCHUNK), :]

        out = [(functools.partial(band_rows, k_ref, c0), functools.partial(band_rows, v_ref, c0),
                functools.partial(bias, c0)) for c0 in range(0, NB_KROWS * GRID_W, KCHUNK)]
        out += [(functools.partial(lambda c0: ck_ref[hh, c0:c0 + KCHUNK, :], c0),
                 functools.partial(lambda c0: cv_ref[hh, c0:c0 + KCHUNK, :], c0), None)
                for c0 in range(0, n_ctx, KCHUNK)]
        return out

    def store(n, pos, pv):
        hh, band = place(n, pos)
        half = hh * HEAD_DIM
        o_ref[pl.ds(pl.multiple_of(band * tq, tq), tq), half:half + HEAD_DIM] = (
            _normalised(pv, hh == 1)[:, half:half + HEAD_DIM].astype(BF16))

    _attention_pipeline(2 * n_bands, PIPE_UNROLL, q_of, chunks_of, store, (s0_ref, s1_ref), mx_ref, pv_ref)


def _nb_attn(tok, q, k, v, ck, cv, rel, rowmask, e):
    rows = tok.ls // GRID_W
    tq = NB_BAND * GRID_W
    kv0 = tok.tp // tok.ls
    past = ck.shape[2]
    assert tok.tp % tok.ls == 0 and past % KCHUNK == 0 and (rows // NB_BAND) % 2 == 0
    return pl.pallas_call(
        functools.partial(_nb_attn_kernel, rows=rows, n_ctx=past),
        out_shape=jax.ShapeDtypeStruct((tok.ts, NB_HEADS * HEAD_DIM), BF16),
        grid=(tok.bs, NB_HEADS // 2),
        in_specs=[
            pl.BlockSpec((2, tok.ls, HEAD_DIM), lambda b, p: (p, kv0 + b, 0)),
            pl.BlockSpec((2, tok.ls, HEAD_DIM), lambda b, p: (p, kv0 + b, 0)),
            pl.BlockSpec((2, tok.ls, 2 * HEAD_DIM), lambda b, p: (p, kv0 + b, 0)),
            pl.BlockSpec((None, 2, past, HEAD_DIM), lambda b, p: (b, p, 0, 0)),
            pl.BlockSpec((None, 2, past, 2 * HEAD_DIM), lambda b, p: (b, p, 0, 0)),
            pl.BlockSpec((None,) + rel.shape[1:4] + (2, GRID_W, LANES), lambda b, p: (e, 0, 0, 0, p, 0, 0)),
            _resident(rowmask.shape, lambda b, p: (0, 0, 0)),
        ],
        out_specs=pl.BlockSpec((tok.ls, 2 * HEAD_DIM), lambda b, p: (b, p)),
        scratch_shapes=_pipeline_scratch(tq, NB_KROWS * GRID_W + past),
        compiler_params=_cparams("arbitrary", "arbitrary"),
        name="nb_attn",
    )(q, k, v, ck, cv, rel, rowmask)


def _nb_band(band, rows):
    n_bands = rows // NB_BAND
    ty = 0 if band == 0 else (2 if band == n_bands - 1 else 1)
    row0 = int(np.clip(NB_BAND * band - NB_ROWS // 2, 0, rows - NB_KROWS))
    return ty, row0, row0 - NB_BAND * band + NB_ROWS - 1


def _nb_row_valid(rows):
    n_bands = rows // NB_BAND
    valid = np.zeros((3, NB_BAND, NB_KROWS), bool)
    for band in (0, 1, n_bands - 1):
        ty, row0, _ = _nb_band(band, rows)
        for jq in range(NB_BAND):
            r = NB_BAND * band + jq
            rs = int(np.clip(r - NB_ROWS // 2, 0, rows - NB_ROWS))
            for m in range(NB_KROWS):
                valid[ty, jq, m] = rs <= row0 + m < rs + NB_ROWS
    return valid


def _toeplitz_cols(rpb):
    w = GRID_W
    n_edge = w - NB_COLS
    lead = rpb.shape[:-1]
    e = jnp.concatenate([jnp.broadcast_to(rpb[..., :1], lead + (n_edge,)), rpb,
                         jnp.broadcast_to(rpb[..., -1:], lead + (n_edge + 1,))], axis=-1)
    flat = jnp.tile(e, (1,) * len(lead) + (w,))[..., :w * (2 * w - 1)]
    return flat.reshape(lead + (w, 2 * w - 1))[..., w - 1:]


def _nb_bias_tables(rpb, rows):
    n_layers, n_heads = rpb.shape[:2]
    n_bands = rows // NB_BAND
    cq = np.arange(GRID_W)
    cstart = np.clip(cq - NB_COLS // 2, 0, GRID_W - NB_COLS)
    col_valid = (cq[None, :] >= cstart[:, None]) & (cq[None, :] < cstart[:, None] + NB_COLS)
    blocks = jnp.where(col_valid, _toeplitz_cols(rpb) * LOG2E, NEG_INF)
    n_d = 2 * NB_ROWS - 1
    blocks = jnp.concatenate([blocks, jnp.zeros((n_layers, n_heads, 1, GRID_W, GRID_W), F32)], axis=2)
    offs = np.array([_nb_band(band, rows)[2] for band in (0, 1, n_bands - 1)])
    n_strip = NB_BAND + NB_KROWS
    d = (offs[:, None, None] - (NB_BAND - 1) + np.arange(2)[None, :, None]
         + np.arange(n_strip)[None, None, :])
    d = np.where((d >= 0) & (d < n_d), d, n_d)
    by_offset = jnp.transpose(blocks, (0, 2, 1, 3, 4))
    halves = [jnp.take(by_offset, jnp.asarray(d[:, :, half::2].reshape(-1)), axis=1) for half in range(2)]
    rel = jnp.concatenate(halves, axis=-1).reshape(n_layers, 3, 2, n_strip // 2, n_heads, GRID_W, 2 * GRID_W)
    small = jnp.where(jnp.asarray(_nb_row_valid(rows)), 0.0, NEG_INF).astype(F32)
    rowmask = jnp.broadcast_to(small[:, :, None, :, None], (3, NB_BAND, GRID_W, NB_KROWS, GRID_W))
    return rel, rowmask.reshape(3, NB_BAND * GRID_W, NB_KROWS * GRID_W)


def _gqa_attn_kernel(q_ref, k_ref, v_ref, ck_ref, cv_ref, o_ref, s0_ref, s1_ref, mx_ref, pv_ref,
                     *, group, n_ctx, n_self):
    def q_of(n, pos):
        r0 = pl.multiple_of(n * GQ_TOK, GQ_TOK)
        return jnp.concatenate([q_ref[g, pl.ds(r0, GQ_TOK), :] for g in range(group)], axis=0)

    def chunks_of(n, pos):
        out = [(functools.partial(lambda c0: ck_ref[0, c0:c0 + KCHUNK, :], c0),
                functools.partial(lambda c0: cv_ref[0, c0:c0 + KCHUNK, :], c0), None)
               for c0 in range(0, n_ctx, KCHUNK)]
        out += [(functools.partial(lambda c0: k_ref[0, c0:c0 + KCHUNK, :], c0),
                 functools.partial(lambda c0: v_ref[0, c0:c0 + KCHUNK, :], c0), None)
                for c0 in range(0, n_self, KCHUNK)]
        return out

    def store(n, pos, pv):
        r0 = pl.multiple_of(n * GQ_TOK, GQ_TOK)
        lower, upper = _normalised(pv, False), _normalised(pv, True)
        pairs = [_pair_lanes(lower[g * GQ_TOK:(g + 1) * GQ_TOK], upper[(g + 1) * GQ_TOK:(g + 2) * GQ_TOK])
                 for g in range(0, group, 2)]
        o_ref[pl.ds(r0, GQ_TOK), :] = jnp.concatenate(pairs, axis=-1).astype(BF16)

    _attention_pipeline(q_ref.shape[1] // GQ_TOK, PIPE_UNROLL, q_of, chunks_of, store,
                        (s0_ref, s1_ref), mx_ref, pv_ref)


def _gqa_attn(tok, q, k, v, ck, cv):
    group = C_Q_HEADS // C_KV_HEADS
    kv0 = tok.tp // tok.ls
    past = ck.shape[2]
    assert past % KCHUNK == 0 and tok.ls % KCHUNK == 0 and tok.ls % (GQ_TOK * PIPE_UNROLL) == 0 and group % 2 == 0
    return pl.pallas_call(
        functools.partial(_gqa_attn_kernel, group=group, n_ctx=past, n_self=tok.ls),
        out_shape=jax.ShapeDtypeStruct((tok.ts, C_Q_HEADS * HEAD_DIM), BF16),
        grid=(tok.bs, C_KV_HEADS),
        in_specs=[
            pl.BlockSpec((group, tok.ls, HEAD_DIM), lambda b, g: (g, kv0 + b, 0)),
            pl.BlockSpec((1, tok.ls, HEAD_DIM), lambda b, g: (g, kv0 + b, 0)),
            pl.BlockSpec((1, tok.ls, 2 * HEAD_DIM), lambda b, g: (g, kv0 + b, 0)),
            pl.BlockSpec((None, 1, past, HEAD_DIM), lambda b, g: (b, g, 0, 0)),
            pl.BlockSpec((None, 1, past, 2 * HEAD_DIM), lambda b, g: (b, g, 0, 0)),
        ],
        out_specs=pl.BlockSpec((tok.ls, group * HEAD_DIM), lambda b, g: (b, g)),
        scratch_shapes=_pipeline_scratch(group * GQ_TOK, past + tok.ls),
        compiler_params=_cparams("arbitrary", "arbitrary"),
        name="gqa_attn",
    )(q, k, v, ck, cv)


def _partner(x, s, lane):
    return jnp.where((lane & s) != 0, pltpu.roll(x, s, 1), pltpu.roll(x, LANES - s, 1))


def _head_rms(x, g, seg_ones):
    y = x * x
    hi = y.astype(BF16)
    lo = (y - hi.astype(F32)).astype(BF16)
    ss = _dot(hi, seg_ones) + _dot(lo, seg_ones)
    return x * lax.rsqrt(ss * (1.0 / HEAD_DIM) + EPS) * g


def _rope(x, cos, sin, lane):
    return x * cos + _partner(x, HEAD_DIM // 4, lane) * sin


def _proj_c_kernel(x_ref, mod_ref, g_ref, w_ref, gq_ref, gk_ref, cos_ref, sin_ref,
                   q_ref, k_ref, v_ref, kf_ref, vf_ref, *, npt):
    i = pl.program_id(0)
    h = _norm_mod(x_ref[...], g_ref[...], mod_ref[3:4, :], mod_ref[4:5, :]).astype(BF16)
    proj = _dot(h, w_ref[...])
    nq = C_Q_HEADS * HEAD_DIM
    nk = C_KV_HEADS * HEAD_DIM
    lane = lax.broadcasted_iota(jnp.int32, (TM, LANES), 1)
    seg_ones = (lax.broadcasted_iota(jnp.int32, (LANES, LANES), 0) // HEAD_DIM ==
                lax.broadcasted_iota(jnp.int32, (LANES, LANES), 1) // HEAD_DIM).astype(BF16)
    cos = cos_ref[...]
    sin = sin_ref[...]
    gq = gq_ref[...]
    gk = gk_ref[...]
    for c in range(nq // LANES):
        y = _head_rms(proj[:, c * LANES:(c + 1) * LANES], gq, seg_ones)
        y = _rope(y, cos, sin, lane) * Q_SCALE
        q_ref[2 * c] = y[:, :HEAD_DIM].astype(BF16)
        q_ref[2 * c + 1] = y[:, HEAD_DIM:].astype(BF16)
    for c in range(nk // LANES):
        yk = _head_rms(proj[:, nq + c * LANES:nq + (c + 1) * LANES], gk, seg_ones)

        @pl.when(i < npt)
        def _(yk=yk, c=c):
            kf_ref[:, c * LANES:(c + 1) * LANES] = yk

        yr = _rope(yk, cos, sin, lane)
        k_ref[2 * c] = yr[:, :HEAD_DIM].astype(BF16)
        k_ref[2 * c + 1] = yr[:, HEAD_DIM:].astype(BF16)
    for hh in range(C_KV_HEADS):
        v_ref[hh] = _with_ones(proj[:, nq + nk + hh * HEAD_DIM:nq + nk + (hh + 1) * HEAD_DIM])

    @pl.when(i < npt)
    def _():
        vf_ref[...] = proj[:, nq + nk:nq + 2 * nk]


def _rope_tables(tok):
    n = HEAD_DIM // 4
    inv = ROPE_THETA ** (-jnp.arange(n, dtype=F32) / n)
    t = jnp.arange(tok.ls)
    pos = jnp.stack([(t // GRID_W).astype(F32), (t % GRID_W).astype(F32)], axis=1)
    ang = pos[:, :, None] * inv[None, None, :]
    cos = jnp.cos(ang)
    sin = jnp.sin(ang)
    cos_h = jnp.concatenate([cos, cos], axis=-1).reshape(tok.ls, HEAD_DIM)
    sin_h = jnp.concatenate([-sin, sin], axis=-1).reshape(tok.ls, HEAD_DIM)
    cos_t = jnp.concatenate([jnp.ones((TM, HEAD_DIM), F32), cos_h], axis=0)
    sin_t = jnp.concatenate([jnp.zeros((TM, HEAD_DIM), F32), sin_h], axis=0)
    return jnp.tile(cos_t, (1, 2)), jnp.tile(sin_t, (1, 2))


def _proj_c(tok, x, mod, g_norm, w, g_q, g_k, cos_t, sin_t, layer):
    o = layer // 2
    nk = C_KV_HEADS * HEAD_DIM

    def rope_idx(i):
        return (jnp.where(i < tok.npt, 0, 1 + (i - tok.npt) % tok.tiles_per_seq), 0)

    (qs, qsh), (ks, ksh), (vs, vsh) = (_head_major(C_Q_HEADS, tok), _head_major(C_KV_HEADS, tok),
                                       _head_major(C_KV_HEADS, tok, 2 * HEAD_DIM))
    vec = pl.BlockSpec((None, 1, LANES), lambda i: (o, 0, 0))
    n = g_q.shape[0]
    return pl.pallas_call(
        functools.partial(_proj_c_kernel, npt=tok.npt),
        out_shape=(qsh, ksh, vsh, jax.ShapeDtypeStruct((tok.tp, nk), F32), jax.ShapeDtypeStruct((tok.tp, nk), F32)),
        grid=(tok.nt,),
        in_specs=[tok.x_spec(), tok.mod_spec(layer), _vec_spec(layer, 1),
                  _resident((None,) + w.shape[1:], lambda i: (o, 0, 0)), vec, vec,
                  pl.BlockSpec((TM, LANES), rope_idx), pl.BlockSpec((TM, LANES), rope_idx)],
        out_specs=(qs, ks, vs, tok.prompt_spec(nk), tok.prompt_spec(nk)),
        compiler_params=_cparams("arbitrary"),
        name="proj_c",
    )(x, mod, g_norm, w, jnp.tile(g_q, (1, 2)).reshape(n, 1, LANES), jnp.tile(g_k, (1, 2)).reshape(n, 1, LANES),
      cos_t, sin_t)


def _ctx_heads(cache, with_ones=False):
    t = jnp.transpose(cache, (0, 2, 1, 3))
    if with_ones:
        t = jnp.concatenate([t, jnp.ones_like(t)], axis=-1)
    return t.astype(BF16)


def kernel(x_prompt, x_sample, cache_nb_k, cache_nb_v, cache_attn_k, cache_attn_v, c, c_ctx, w_mod, b_mod, g_norm, w_ffn_in, w_ffn_out, w_in_ab, w_pool, pool_scale, nb_rpb, w_out_ab, w_qkv_c, g_qnorm, g_knorm, w_out_c, g_final):
    bp, lp, d = x_prompt.shape
    bs, ls, _ = x_sample.shape
    assert d == D_MODEL and w_ffn_in.shape[-1] == 2 * D_FF and w_mod.shape[0] == DEPTH
    tok = _Tokens(bp, lp, bs, ls)

    n_rows = SUBLANES * (-(-(1 + bs) // SUBLANES))
    cvec = jnp.concatenate([c_ctx[None], c, jnp.zeros((n_rows - 1 - bs, d), F32)], axis=0)
    mod = _ada(cvec, w_mod, b_mod).reshape(DEPTH, n_rows, N_MOD, d)
    g_norm4 = g_norm.reshape(DEPTH, 3, 1, d)

    w_in_ab_b = w_in_ab.astype(BF16)
    w_pool_b = w_pool.astype(BF16)
    w_out_ab_b = w_out_ab.astype(BF16)
    w_qkv_c_b = w_qkv_c.astype(BF16)
    w_out_c_b = w_out_c.astype(BF16)
    cos_t, sin_t = _rope_tables(tok)
    rel, rowmask = _nb_bias_tables(nb_rpb, ls // GRID_W)

    x = (x_prompt.reshape(tok.tp, d), x_sample.reshape(tok.ts, d))
    nb_k, nb_v, at_k, at_v = [], [], [], []
    for l in range(DEPTH):
        x = _ffn(tok, x, mod, g_norm4, w_ffn_in, w_ffn_out, l, 0)
        if l % 2 == 0:
            e = l // 2
            u, q, k, v, kf, vf = _proj_ab(tok, x, mod, g_norm4, w_in_ab_b, l)
            a = _pool(tok, u, w_pool_b, pool_scale, e)
            b_p = _prompt_attn(tok, q, k, v, NB_HEADS, NB_HEADS)
            b_s = _nb_attn(tok, q, k, v, _ctx_heads(cache_nb_k[:, e]), _ctx_heads(cache_nb_v[:, e], True),
                           rel, rowmask, e)
            mixer = (a, b_p, b_s, w_out_ab_b)
            nb_k.append(kf.reshape(bp, lp, NB_HEADS, HEAD_DIM))
            nb_v.append(vf.reshape(bp, lp, NB_HEADS, HEAD_DIM))
        else:
            o = l // 2
            q, k, v, kf, vf = _proj_c(tok, x, mod, g_norm4, w_qkv_c_b, g_qnorm, g_knorm, cos_t, sin_t, l)
            b_p = _prompt_attn(tok, q, k, v, C_Q_HEADS, C_KV_HEADS)
            b_s = _gqa_attn(tok, q, k, v, _ctx_heads(cache_attn_k[:, o]), _ctx_heads(cache_attn_v[:, o], True))
            mixer = (None, b_p, b_s, w_out_c_b)
            at_k.append(kf.reshape(bp, lp, C_KV_HEADS, HEAD_DIM))
            at_v.append(vf.reshape(bp, lp, C_KV_HEADS, HEAD_DIM))
        x = _ffn(tok, x, mod, g_norm4, w_ffn_in, w_ffn_out, l, 2, g_final=g_final if l == DEPTH - 1 else None,
                 mixer=mixer)

    y_prompt, y_sample = x
    return (y_prompt.reshape(bp, lp, d), y_sample.reshape(bs, ls, d), jnp.stack(nb_k, axis=1),
            jnp.stack(nb_v, axis=1), jnp.stack(at_k, axis=1), jnp.stack(at_v, axis=1))
```
